```python
import math
import jax, jax.numpy as jnp
from jax import lax
import numpy as np

D_MODEL = 1024
BATCH = 8
SEQ = 4096
DEPTH = 2

CTX_LEN = 256
GRID_W = 64
N_EVEN = (DEPTH + 1) // 2
N_ODD = DEPTH // 2

DEEPNORM_ALPHA = (2.0 * DEPTH) ** 0.25
DEEPNORM_BETA = (8.0 * DEPTH) ** -0.25
LN_EPS = 1e-5
RMS_EPS = 1e-6
N_MOD = 9

D_FF = 2816
MACARON_WEIGHT = 0.5

GDN_HEADS = 4
GDN_HEAD_DIM = 128
GDN_WIDTH = GDN_HEADS * GDN_HEAD_DIM
GDN_CONV = 5
GDN_CHUNK = 64
CONV_CH = D_MODEL // 2
CONV_WIDTH = 31
EVEN_SPLITS = (3 * GDN_WIDTH, 4 * GDN_WIDTH, 4 * GDN_WIDTH + 4 * GDN_HEADS)
EVEN_IN = 4 * GDN_WIDTH + 4 * GDN_HEADS + 2 * CONV_CH
EVEN_OUT = GDN_WIDTH + CONV_CH

ATT_HEADS = 8
ATT_KV_HEADS = 2
ATT_GROUP = ATT_HEADS // ATT_KV_HEADS
ATT_HEAD_DIM = 128
ATT_BLOCK = 128
ROPE_THETA = 10000.0
ODD_IN = (ATT_HEADS + 2 * ATT_KV_HEADS) * ATT_HEAD_DIM
ODD_OUT = ATT_HEADS * ATT_HEAD_DIM

kernel_name = "hybrid_gdn_conformer_gqa_dit_block"


def layer_norm(x, g, b):
    xf = x.astype(jnp.float32)
    mu = xf.mean(-1, keepdims=True)
    var = jnp.square(xf - mu).mean(-1, keepdims=True)
    return ((xf - mu) * lax.rsqrt(var + LN_EPS) * g + b).astype(x.dtype)


def rms_norm(x, g):
    xf = x.astype(jnp.float32)
    return (xf * lax.rsqrt(jnp.square(xf).mean(-1, keepdims=True) + RMS_EPS) * g).astype(x.dtype)


def l2_normalize(x):
    xf = x.astype(jnp.float32)
    return xf * lax.rsqrt(jnp.square(xf).sum(-1, keepdims=True) + RMS_EPS)


def modulate(x, shift, scale):
    return x * (1 + scale) + shift


def deepnorm_residual(x, y, g, b):
    return layer_norm(DEEPNORM_ALPHA * x + y, g, b)


def adaln_modulation(cond, w_ada, b_ada):
    return jnp.split(jax.nn.silu(cond) @ w_ada + b_ada, N_MOD, axis=-1)


def swiglu(x, w_gu, w_down):
    gate, up = jnp.split(x @ w_gu, 2, axis=-1)
    return (jax.nn.silu(gate) * up) @ w_down


def ffn_sublayer(x, shift, scale, gate, w_gu, w_down, g, b):
    y = swiglu(modulate(x, shift, scale), w_gu, w_down)
    return deepnorm_residual(x, MACARON_WEIGHT * gate * y, g, b)


def depthwise_conv(x, w):
    k = w.shape[0]
    return lax.conv_general_dilated(
        x, w[:, None, :], window_strides=(1,), padding=[(k // 2, k // 2)],
        dimension_numbers=('NWC', 'WIO', 'NWC'), feature_group_count=x.shape[-1])


def axial_rope_tables(rows, head_dim, dtype):
    row = jnp.repeat(jnp.arange(rows, dtype=jnp.float32), GRID_W)
    col = jnp.tile(jnp.arange(GRID_W, dtype=jnp.float32), rows)
    n_freq = head_dim // 4
    inv_freq = jnp.float32(ROPE_THETA) ** (-jnp.arange(n_freq, dtype=jnp.float32) / n_freq)
    ang = jnp.concatenate([row[:, None] * inv_freq, col[:, None] * inv_freq], axis=-1)
    return jnp.cos(ang).astype(dtype), jnp.sin(ang).astype(dtype)


def apply_rope(x, cos, sin):
    x1, x2 = x[..., 0::2], x[..., 1::2]
    c, s = cos[:, None, :], sin[:, None, :]
    return jnp.stack([x1 * c - x2 * s, x1 * s + x2 * c], axis=-1).reshape(x.shape)


def gated_delta_chunked(q, k, v, g, beta, state0):
    B, L, H, dk = q.shape
    dv = v.shape[-1]
    C = GDN_CHUNK
    n = L // C
    f32 = jnp.float32

    def to_chunks(t):
        t = t.astype(f32).reshape((B, n, C, H) + t.shape[3:])
        return jnp.moveaxis(t, (1, 3), (0, 2))

    qc = to_chunks(q) * (dk ** -0.5)
    kc, vc = to_chunks(k), to_chunks(v)
    gcum = jnp.cumsum(to_chunks(g), axis=-1)
    bc = to_chunks(beta)
    tril = jnp.tril(jnp.ones((C, C), dtype=bool))
    strict = jnp.tril(jnp.ones((C, C), dtype=bool), -1)
    diff = gcum[..., :, None] - gcum[..., None, :]
    decay = jnp.where(tril, jnp.exp(jnp.where(tril, diff, 0.0)), 0.0)
    k_beta = kc * bc[..., None]
    v_beta = vc * bc[..., None]
    m = jnp.where(strict, jnp.einsum('nbhik,nbhjk->nbhij', k_beta, kc) * decay, 0.0)
    a = jnp.eye(C, dtype=f32) + m
    u = lax.linalg.triangular_solve(a, v_beta, left_side=True, lower=True, unit_diagonal=True)
    w = lax.linalg.triangular_solve(a, k_beta * jnp.exp(gcum)[..., None],
                                    left_side=True, lower=True, unit_diagonal=True)
    attn_intra = jnp.where(tril, jnp.einsum('nbhik,nbhjk->nbhij', qc, kc) * decay, 0.0)
    g_last = gcum[..., -1]
    k_tail = kc * jnp.exp(g_last[..., None] - gcum)[..., None]

    def step(S, inp):
        q_i, u_i, w_i, at_i, gc_i, gl_i, kt_i = inp
        v_new = u_i - w_i @ S
        o = (q_i * jnp.exp(gc_i)[..., None]) @ S + at_i @ v_new
        S = S * jnp.exp(gl_i)[..., None, None] + jnp.swapaxes(kt_i, -1, -2) @ v_new
        return S, o

    S, o = lax.scan(step, state0.astype(f32), (qc, u, w, attn_intra, gcum, g_last, k_tail))
    o = jnp.moveaxis(o, (0, 2), (1, 3)).reshape(B, L, H, dv)
    return o, S


def gdn_bidirectional(q, k, v, g_f, g_b, b_f, b_b, s_f0, s_b0):
    flip = lambda t: jnp.flip(t, axis=1)
    o_f, s_f = gated_delta_chunked(q, k, v, g_f, b_f, s_f0)
    o_b, s_b = gated_delta_chunked(flip(q), flip(k), flip(v), flip(g_b), flip(b_b), s_b0)
    return o_f + flip(o_b), s_f, s_b


def gdn_gates(ab, a_log, dt_bias):
    B, L, _ = ab.shape
    af = ab.astype(jnp.float32)
    a = af[..., :2 * GDN_HEADS].reshape(B, L, 2, GDN_HEADS)
    b = af[..., 2 * GDN_HEADS:].reshape(B, L, 2, GDN_HEADS)
    g = -jnp.exp(a_log.astype(jnp.float32)) * jax.nn.softplus(a + dt_bias.astype(jnp.float32))
    beta = jax.nn.sigmoid(b)
    return g[:, :, 0], g[:, :, 1], beta[:, :, 0], beta[:, :, 1]


def even_mixer(h_lat, h_ctx, w_in, qkv_conv, a_log, dt_bias, out_norm,
               dw_conv, dw_bias, cln_g, cln_b, w_out, need_ctx):
    def project(h):
        B, L, _ = h.shape
        qkv, z, ab, glu = jnp.split(h @ w_in, EVEN_SPLITS, axis=-1)
        qkv = jax.nn.silu(depthwise_conv(qkv, qkv_conv))
        q, k, v = [t.reshape(B, L, GDN_HEADS, GDN_HEAD_DIM) for t in jnp.split(qkv, 3, axis=-1)]
        scan_in = (l2_normalize(q), l2_normalize(k), v) + gdn_gates(ab, a_log, dt_bias)
        return scan_in, z, glu

    def conformer(glu):
        val, gate = jnp.split(glu, 2, axis=-1)
        y = depthwise_conv(val * jax.nn.sigmoid(gate), dw_conv) + dw_bias
        return jax.nn.silu(layer_norm(y, cln_g, cln_b))

    def merge(o, z, glu):
        B, L, _ = z.shape
        o = rms_norm(o, out_norm).astype(z.dtype) * jax.nn.silu(z.reshape(B, L, GDN_HEADS, GDN_HEAD_DIM))
        return jnp.concatenate([o.reshape(B, L, GDN_WIDTH), conformer(glu)], axis=-1) @ w_out

    ctx_in, z_c, glu_c = project(h_ctx)
    lat_in, z_l, glu_l = project(h_lat)
    zeros = jnp.zeros((h_ctx.shape[0], GDN_HEADS, GDN_HEAD_DIM, GDN_HEAD_DIM), jnp.float32)
    o_c, s_f, s_b = gdn_bidirectional(*ctx_in, zeros, zeros)
    o_l, _, _ = gdn_bidirectional(*lat_in, s_f, s_b)
    lat_out = merge(o_l, z_l, glu_l)
    ctx_out = merge(o_c, z_c, glu_c) if need_ctx else None
    return lat_out, ctx_out


def gqa_attend(q, k, v):
    s = jnp.einsum('bqhgd,blhd->bhgql', q, k).astype(jnp.float32) * (ATT_HEAD_DIM ** -0.5)
    p = jax.nn.softmax(s, axis=-1).astype(v.dtype)
    return jnp.einsum('bhgql,blhd->bqhgd', p, v)


def odd_mixer(h_lat, h_ctx, w_in, q_norm, k_norm, w_out, rope_cos, rope_sin, need_ctx):
    hd = ATT_HEAD_DIM

    def project(h):
        B, L, _ = h.shape
        q, k, v = jnp.split(h @ w_in, [ATT_HEADS * hd, (ATT_HEADS + ATT_KV_HEADS) * hd], axis=-1)
        q = rms_norm(q.reshape(B, L, ATT_HEADS, hd), q_norm)
        k = rms_norm(k.reshape(B, L, ATT_KV_HEADS, hd), k_norm)
        return q, k, v.reshape(B, L, ATT_KV_HEADS, hd)

    q_c, k_c, v_c = project(h_ctx)
    q_l, k_l, v_l = project(h_lat)
    q_l = apply_rope(q_l, rope_cos, rope_sin)
    k_l = apply_rope(k_l, rope_cos, rope_sin)
    B, L = h_lat.shape[:2]
    k_all = jnp.concatenate([k_l, k_c], axis=1)
    v_all = jnp.concatenate([v_l, v_c], axis=1)
    n_blk = L // ATT_BLOCK
    q_blocks = q_l.reshape(B, n_blk, ATT_BLOCK, ATT_KV_HEADS, ATT_GROUP, hd).swapaxes(0, 1)
    o = lax.map(lambda qb: gqa_attend(qb, k_all, v_all), q_blocks)
    lat_out = o.swapaxes(0, 1).reshape(B, L, ODD_OUT) @ w_out
    ctx_out = None
    if need_ctx:
        Bc, Lc = h_ctx.shape[:2]
        o_c = gqa_attend(q_c.reshape(Bc, Lc, ATT_KV_HEADS, ATT_GROUP, hd), k_c, v_c)
        ctx_out = o_c.reshape(Bc, Lc, ODD_OUT) @ w_out
    return lat_out, ctx_out


def setup_inputs(seed: int = 0) -> dict:
    key = jax.random.key(seed)
    ks = jax.random.split(key, 32)
    f32 = jnp.float32
    nrm = lambda k, shape, s: jax.random.normal(k, shape, f32) * s
    dt = jnp.exp(jax.random.uniform(ks[13], (N_EVEN, 2, GDN_HEADS), f32, math.log(1e-3), math.log(1e-1)))
    return {
        "x": nrm(ks[0], (BATCH, SEQ, D_MODEL), 1.0),
        "c": nrm(ks[1], (BATCH, D_MODEL), 1.0),
        "ctx": nrm(ks[2], (BATCH, CTX_LEN, D_MODEL), 1.0),
        "c_ctx": nrm(ks[3], (D_MODEL,), 1.0),
        "ada_w": nrm(ks[4], (DEPTH, D_MODEL, N_MOD * D_MODEL), 0.5 * D_MODEL ** -0.5),
        "ada_b": nrm(ks[5], (DEPTH, N_MOD * D_MODEL), 0.02),
        "ln_g": 1.0 + nrm(ks[6], (DEPTH, 3, D_MODEL), 0.02),
        "ln_b": nrm(ks[7], (DEPTH, 3, D_MODEL), 0.02),
        "ffn_w_gu": nrm(ks[8], (DEPTH, 2, D_MODEL, 2 * D_FF), D_MODEL ** -0.5),
        "ffn_w_down": nrm(ks[9], (DEPTH, 2, D_FF, D_MODEL), DEEPNORM_BETA * D_FF ** -0.5),
        "even_w_in": nrm(ks[10], (N_EVEN, D_MODEL, EVEN_IN), D_MODEL ** -0.5),
        "even_qkv_conv": nrm(ks[11], (N_EVEN, GDN_CONV, 3 * GDN_WIDTH), GDN_CONV ** -0.5),
        "gdn_a_log": jnp.log(jax.random.uniform(ks[12], (N_EVEN, 2, GDN_HEADS), f32, 1.0, 16.0)),
        "gdn_dt_bias": dt + jnp.log(-jnp.expm1(-dt)),
        "gdn_out_norm": 1.0 + nrm(ks[14], (N_EVEN, GDN_HEAD_DIM), 0.02),
        "cf_dw_conv": nrm(ks[15], (N_EVEN, CONV_WIDTH, CONV_CH), CONV_WIDTH ** -0.5),
        "cf_dw_bias": nrm(ks[16], (N_EVEN, CONV_CH), 0.02),
        "cf_ln_g": 1.0 + nrm(ks[17], (N_EVEN, CONV_CH), 0.02),
        "cf_ln_b": nrm(ks[18], (N_EVEN, CONV_CH), 0.02),
        "even_w_out": nrm(ks[19], (N_EVEN, EVEN_OUT, D_MODEL), DEEPNORM_BETA * EVEN_OUT ** -0.5),
        "attn_w_in": nrm(ks[20], (N_ODD, D_MODEL, ODD_IN), D_MODEL ** -0.5),
        "attn_q_norm": 1.0 + nrm(ks[21], (N_ODD, ATT_HEAD_DIM), 0.02),
        "attn_k_norm": 1.0 + nrm(ks[22], (N_ODD, ATT_HEAD_DIM), 0.02),
        "attn_w_out": nrm(ks[23], (N_ODD, ODD_OUT, D_MODEL), DEEPNORM_BETA * ODD_OUT ** -0.5),
    }


def reference(x, c, ctx, c_ctx, ada_w, ada_b, ln_g, ln_b, ffn_w_gu, ffn_w_down,
              even_w_in, even_qkv_conv, gdn_a_log, gdn_dt_bias, gdn_out_norm,
              cf_dw_conv, cf_dw_bias, cf_ln_g, cf_ln_b, even_w_out,
              attn_w_in, attn_q_norm, attn_k_norm, attn_w_out):
    rows = x.shape[1] // GRID_W
    rope_cos, rope_sin = axial_rope_tables(rows, ATT_HEAD_DIM, x.dtype)
    for layer in range(DEPTH):
        last = layer == DEPTH - 1
        i = layer // 2
        mods_lat = [m[:, None, :] for m in adaln_modulation(c, ada_w[layer], ada_b[layer])]
        mods_ctx = adaln_modulation(c_ctx, ada_w[layer], ada_b[layer])
        x = ffn_sublayer(x, *mods_lat[0:3], ffn_w_gu[layer, 0], ffn_w_down[layer, 0], ln_g[layer, 0], ln_b[layer, 0])
        ctx = ffn_sublayer(ctx, *mods_ctx[0:3], ffn_w_gu[layer, 0], ffn_w_down[layer, 0], ln_g[layer, 0], ln_b[layer, 0])
        h_lat = modulate(x, *mods_lat[3:5])
        h_ctx = modulate(ctx, *mods_ctx[3:5])
        if layer % 2 == 0:
            o_lat, o_ctx = even_mixer(h_lat, h_ctx, even_w_in[i], even_qkv_conv[i], gdn_a_log[i], gdn_dt_bias[i],
                                      gdn_out_norm[i], cf_dw_conv[i], cf_dw_bias[i], cf_ln_g[i], cf_ln_b[i],
                                      even_w_out[i], not last)
        else:
            o_lat, o_ctx = odd_mixer(h_lat, h_ctx, attn_w_in[i], attn_q_norm[i], attn_k_norm[i], attn_w_out[i],
                                     rope_cos, rope_sin, not last)
        x = deepnorm_residual(x, mods_lat[5] * o_lat, ln_g[layer, 1], ln_b[layer, 1])
        x = ffn_sublayer(x, *mods_lat[6:9], ffn_w_gu[layer, 1], ffn_w_down[layer, 1], ln_g[layer, 2], ln_b[layer, 2])
        if not last:
            ctx = deepnorm_residual(ctx, mods_ctx[5] * o_ctx, ln_g[layer, 1], ln_b[layer, 1])
            ctx = ffn_sublayer(ctx, *mods_ctx[6:9], ffn_w_gu[layer, 1], ffn_w_down[layer, 1], ln_g[layer, 2], ln_b[layer, 2])
    return x
```

```python
import functools

import jax
import jax.numpy as jnp
from jax import lax
from jax.experimental import pallas as pl
from jax.experimental.pallas import tpu as pltpu

F32 = jnp.float32
BF16 = jnp.bfloat16

GRID_W = 64
LN_EPS = 1e-5
RMS_EPS = 1e-6
N_MOD = 9
MACARON_WEIGHT = 0.5
GDN_HEADS = 4
GDN_HEAD_DIM = 128
GDN_CHUNK = 64
ATT_HEADS = 8
ATT_KV_HEADS = 2
ATT_GROUP = ATT_HEADS // ATT_KV_HEADS
ATT_HEAD_DIM = 128
ROPE_THETA = 10000.0

V7X_LANES = 128
V7X_SUBLANES = 8
V7X_VMEM_LIMIT_BYTES = 56 * 1024 * 1024

COND_ROWS = 16
CONV_PAD = 16
ROW_TILE = 512
CONV_ROWS = 256
GDN_TILE = 512
ATT_Q_TILE = 256


def _sigmoid(x):
    return 1.0 / (1.0 + jnp.exp(-x))


def _silu(x):
    return x * _sigmoid(x)


def _softplus(x):
    return jnp.maximum(x, 0.0) + jnp.log1p(jnp.exp(-jnp.abs(x)))


def _layer_norm(x, g, b):
    mu = jnp.mean(x, axis=-1, keepdims=True)
    xc = x - mu
    var = jnp.mean(xc * xc, axis=-1, keepdims=True)
    return xc * lax.rsqrt(var + LN_EPS) * g + b


def _rms_norm(x, g):
    return x * lax.rsqrt(jnp.mean(x * x, axis=-1, keepdims=True) + RMS_EPS) * g


def _bdot(a, b):
    return jnp.dot(a.astype(BF16), b.astype(BF16), preferred_element_type=F32)


def _bdot_nt(a, b):
    return lax.dot_general(a.astype(BF16), b.astype(BF16), (((1,), (1,)), ((), ())),
                           preferred_element_type=F32)


def _params(*semantics):
    return pltpu.CompilerParams(dimension_semantics=semantics,
                                vmem_limit_bytes=V7X_VMEM_LIMIT_BYTES)


def _resident(shape):
    nd = len(shape)
    return pl.BlockSpec(shape, lambda *_: (0,) * nd, pipeline_mode=pl.Buffered(1))


def _row_tile(m, seq):
    t = min(ROW_TILE, seq)
    assert m % t == 0 and seq % t == 0
    return t


def _ada_kernel(c_ref, w_ref, b_ref, o_ref):
    c = c_ref[...]
    o_ref[0] = _bdot(_silu(c), w_ref[0]) + b_ref[0]


def _ada(cond, ada_w, ada_b):
    depth, d, nd = ada_w.shape
    return pl.pallas_call(
        _ada_kernel,
        grid=(depth, nd // d),
        in_specs=[pl.BlockSpec((COND_ROWS, d), lambda l, j: (0, 0)),
                  pl.BlockSpec((1, d, d), lambda l, j: (l, 0, j)),
                  pl.BlockSpec((1, 1, d), lambda l, j: (l, 0, j))],
        out_specs=pl.BlockSpec((1, COND_ROWS, d), lambda l, j: (l, 0, j)),
        out_shape=jax.ShapeDtypeStruct((depth, COND_ROWS, nd), F32),
        compiler_params=_params("parallel", "parallel"),
        name="ada_modulation",
    )(cond, ada_w, ada_b.reshape(depth, 1, nd))


def _ffn_kernel(x_ref, mods_ref, wgu_ref, wd_ref, lng_ref, lnb_ref, o_ref, *, mod_base, d_ff, alpha):
    x = x_ref[...]
    shift = mods_ref[0, mod_base:mod_base + 1, :]
    scale = mods_ref[0, mod_base + 1:mod_base + 2, :]
    gate = mods_ref[0, mod_base + 2:mod_base + 3, :]
    h = (x * (1.0 + scale) + shift).astype(BF16)
    g = jnp.dot(h, wgu_ref[:, :d_ff], preferred_element_type=F32)
    u = jnp.dot(h, wgu_ref[:, d_ff:], preferred_element_type=F32)
    a = (_silu(g) * u).astype(BF16)
    y = jnp.dot(a, wd_ref[...], preferred_element_type=F32)
    o_ref[...] = _layer_norm(alpha * x + (MACARON_WEIGHT * gate) * y, lng_ref[...], lnb_ref[...])


def _ffn(x2, mods, seq, w_gu, w_down, ln_g, ln_b, mod_base, alpha):
    m, d = x2.shape
    d_ff = w_down.shape[0]
    tm = _row_tile(m, seq)
    rpm = seq // tm if mods.shape[0] > 1 else m // tm
    return pl.pallas_call(
        functools.partial(_ffn_kernel, mod_base=mod_base, d_ff=d_ff, alpha=alpha),
        grid=(m // tm,),
        in_specs=[pl.BlockSpec((tm, d), lambda i: (i, 0)),
                  pl.BlockSpec((1, N_MOD, d), lambda i: (i // rpm, 0, 0)),
                  _resident((d, 2 * d_ff)), _resident((d_ff, d)),
                  _resident((1, d)), _resident((1, d))],
        out_specs=pl.BlockSpec((tm, d), lambda i: (i, 0)),
        out_shape=jax.ShapeDtypeStruct((m, d), F32),
        compiler_params=_params("parallel"),
        name="ffn_sublayer",
    )(x2, mods, w_gu, w_down, ln_g.reshape(1, d), ln_b.reshape(1, d))


def _even_proj_kernel(x_ref, mods_ref, w_ref, alog_ref, dtb_ref, qkv_ref, z_ref, gb_ref, u_ref,
                      *, n_qkv, n_z, n_cf):
    x = x_ref[...]
    shift = mods_ref[0, 3:4, :]
    scale = mods_ref[0, 4:5, :]
    h = (x * (1.0 + scale) + shift).astype(BF16)
    y = jnp.dot(h, w_ref[...], preferred_element_type=F32)
    qkv_ref[...] = y[:, :n_qkv]
    o = n_qkv
    z_ref[...] = y[:, o:o + n_z]
    o += n_z
    ab = y[:, o:o + V7X_LANES]
    o += V7X_LANES
    lane = lax.broadcasted_iota(jnp.int32, ab.shape, 1)
    g = -jnp.exp(alog_ref[...]) * _softplus(ab + dtb_ref[...])
    gb_ref[...] = jnp.where(lane < 2 * GDN_HEADS, g, _sigmoid(ab))
    val = y[:, o:o + n_cf]
    gate = y[:, o + n_cf:o + 2 * n_cf]
    u_ref[...] = val * _sigmoid(gate)


def _even_proj(x2, mods, seq, w_pad, alog_row, dtb_row, n_qkv, n_z, n_cf):
    m, d = x2.shape
    tm = _row_tile(m, seq)
    rpm = seq // tm if mods.shape[0] > 1 else m // tm
    row = lambda w: pl.BlockSpec((tm, w), lambda i: (i, 0))
    return pl.pallas_call(
        functools.partial(_even_proj_kernel, n_qkv=n_qkv, n_z=n_z, n_cf=n_cf),
        grid=(m // tm,),
        in_specs=[row(d), pl.BlockSpec((1, N_MOD, d), lambda i: (i // rpm, 0, 0)),
                  _resident(w_pad.shape), _resident((1, V7X_LANES)), _resident((1, V7X_LANES))],
        out_specs=[row(n_qkv), row(n_z), row(V7X_LANES), row(n_cf)],
        out_shape=[jax.ShapeDtypeStruct((m, n_qkv), F32), jax.ShapeDtypeStruct((m, n_z), F32),
                   jax.ShapeDtypeStruct((m, V7X_LANES), F32), jax.ShapeDtypeStruct((m, n_cf), F32)],
        compiler_params=_params("parallel"),
        name="even_in_proj",
    )(x2, mods, w_pad, alog_row, dtb_row)


def _conv_kernel(x_ref, w_ref, b_ref, o_ref, pad_ref, *, taps, seq, rows, qk_norm, n_q, n_qk):
    half = taps // 2
    zeros = jnp.zeros((CONV_PAD, V7X_LANES), F32)
    pad_ref[0:CONV_PAD, :] = zeros
    pad_ref[CONV_PAD + seq:CONV_PAD + seq + CONV_PAD, :] = zeros
    pad_ref[CONV_PAD:CONV_PAD + seq, :] = x_ref[0]
    j = pl.program_id(1)
    for c in range(seq // rows):
        base = CONV_PAD + c * rows - half
        acc = w_ref[0:1, :] * pad_ref[base:base + rows, :]
        for k in range(1, taps):
            acc = acc + w_ref[k:k + 1, :] * pad_ref[base + k:base + k + rows, :]
        if qk_norm:
            y = _silu(acc)
            inv = lax.rsqrt(jnp.sum(y * y, axis=-1, keepdims=True) + RMS_EPS)
            fac = jnp.where(j < n_q, inv * (GDN_HEAD_DIM ** -0.5), jnp.where(j < n_qk, inv, 1.0))
            out = y * fac
        else:
            out = acc + b_ref[...]
        o_ref[0, c * rows:(c + 1) * rows, :] = out


def _dw_conv(x3, w, bias, qk_norm):
    b, seq, ch = x3.shape
    taps = w.shape[0]
    assert taps // 2 <= CONV_PAD and ch % V7X_LANES == 0
    rows = min(CONV_ROWS, seq)
    return pl.pallas_call(
        functools.partial(_conv_kernel, taps=taps, seq=seq, rows=rows, qk_norm=qk_norm,
                          n_q=GDN_HEADS, n_qk=2 * GDN_HEADS),
        grid=(b, ch // V7X_LANES),
        in_specs=[pl.BlockSpec((1, seq, V7X_LANES), lambda i, j: (i, 0, j)),
                  pl.BlockSpec((taps, V7X_LANES), lambda i, j: (0, j)),
                  pl.BlockSpec((1, V7X_LANES), lambda i, j: (0, j))],
        out_specs=pl.BlockSpec((1, seq, V7X_LANES), lambda i, j: (i, 0, j)),
        out_shape=jax.ShapeDtypeStruct((b, seq, ch), F32),
        scratch_shapes=[pltpu.VMEM((seq + 2 * CONV_PAD, V7X_LANES), F32)],
        compiler_params=_params("parallel", "parallel"),
        name="qkv_conv" if qk_norm else "conformer_conv",
    )(x3, w, bias.reshape(1, ch))


def _gdn_kernel(qkvf_ref, qkvb_ref, gbf_ref, gbb_ref, s0_ref, of_ref, ob_ref, sout_ref, s_ref,
                *, n_chunks):
    nh, dh, ck = GDN_HEADS, GDN_HEAD_DIM, GDN_CHUNK
    width = nh * dh
    t = pl.program_id(1)

    @pl.when(t == 0)
    def _():
        s_ref[...] = s0_ref[0]

    row = lax.broadcasted_iota(jnp.int32, (ck, ck), 0)
    col = lax.broadcasted_iota(jnp.int32, (ck, ck), 1)
    eye = (row == col).astype(F32)
    masks = ((row >= col, row > col), (row <= col, row < col))
    n_double = ck.bit_length() - 2

    def chunk_step(ci, carry):
        for d, (qkv_ref, gb_ref, o_ref) in enumerate(((qkvf_ref, gbf_ref, of_ref),
                                                      (qkvb_ref, gbb_ref, ob_ref))):
            r0 = pl.multiple_of((ci if d == 0 else n_chunks - 1 - ci) * ck, ck)
            mask, smask = masks[d]
            last = ck - 1 if d == 0 else 0
            gbc = gb_ref[0, pl.ds(r0, ck), :]
            gc_all = jnp.dot(mask.astype(F32), gbc, precision=lax.Precision.HIGHEST,
                             preferred_element_type=F32)
            gc_all_t = gc_all.T
            for h in range(nh):
                lane = d * nh + h
                gcol = gc_all[:, lane:lane + 1]
                grow = gc_all_t[lane:lane + 1, :]
                beta = gbc[:, 2 * nh + lane:2 * nh + lane + 1]
                q = qkv_ref[0, pl.ds(r0, ck), h * dh:(h + 1) * dh]
                k = qkv_ref[0, pl.ds(r0, ck), width + h * dh:width + (h + 1) * dh]
                v = qkv_ref[0, pl.ds(r0, ck), 2 * width + h * dh:2 * width + (h + 1) * dh]
                decay = jnp.where(mask, jnp.exp(jnp.where(mask, gcol - grow, 0.0)), 0.0)
                kb = k * beta
                kkqk = _bdot_nt(jnp.concatenate([kb, q], axis=0), k)
                m = jnp.where(smask, kkqk[:ck] * decay, 0.0)
                attn = kkqk[ck:] * decay
                inv = eye - m
                mp = m
                for _ in range(n_double):
                    mp = _bdot(mp, mp)
                    inv = inv + _bdot(inv, mp)
                eg = jnp.exp(gcol)
                uw = _bdot(inv, jnp.concatenate([v * beta, kb * eg], axis=1))
                u = uw[:, :dh]
                w = uw[:, dh:]
                s = s_ref[lane]
                wq_s = _bdot(jnp.concatenate([w, q * eg], axis=0), s)
                v_new = u - wq_s[:ck]
                o_ref[0, pl.ds(r0, ck), h * dh:(h + 1) * dh] = wq_s[ck:] + _bdot(attn, v_new)
                glast = gcol[last:last + 1, :]
                kt = k * jnp.exp(glast - gcol)
                s_ref[lane] = s * jnp.exp(glast) + _bdot(kt.T, v_new)
        return carry

    lax.fori_loop(0, n_chunks, chunk_step, 0)

    @pl.when(t == pl.num_programs(1) - 1)
    def _():
        sout_ref[0] = s_ref[...]


def _gdn_scan(qkv3, gb3, s0):
    b, seq, w3 = qkv3.shape
    nh, dh = GDN_HEADS, GDN_HEAD_DIM
    lt = min(GDN_TILE, seq)
    assert seq % lt == 0 and lt % GDN_CHUNK == 0
    nt = seq // lt
    fwd = lambda i, t: (i, t, 0)
    bwd = lambda i, t: (i, nt - 1 - t, 0)
    st = pl.BlockSpec((1, 2 * nh, dh, dh), lambda i, t: (i, 0, 0, 0))
    return pl.pallas_call(
        functools.partial(_gdn_kernel, n_chunks=lt // GDN_CHUNK),
        grid=(b, nt),
        in_specs=[pl.BlockSpec((1, lt, w3), fwd), pl.BlockSpec((1, lt, w3), bwd),
                  pl.BlockSpec((1, lt, V7X_LANES), fwd), pl.BlockSpec((1, lt, V7X_LANES), bwd), st],
        out_specs=[pl.BlockSpec((1, lt, nh * dh), fwd), pl.BlockSpec((1, lt, nh * dh), bwd), st],
        out_shape=[jax.ShapeDtypeStruct((b, seq, nh * dh), F32),
                   jax.ShapeDtypeStruct((b, seq, nh * dh), F32),
                   jax.ShapeDtypeStruct((b, 2 * nh, dh, dh), F32)],
        scratch_shapes=[pltpu.VMEM((2 * nh, dh, dh), F32)],
        compiler_params=_params("parallel", "arbitrary"),
        name="gdn_scan",
    )(qkv3, qkv3, gb3, gb3, s0)


def _even_merge_kernel(of_ref, ob_ref, z_ref, y_ref, x_ref, mods_ref, onorm_ref, clng_ref, clnb_ref,
                       w1_ref, w2_ref, lng_ref, lnb_ref, o_ref, *, alpha):
    dh = GDN_HEAD_DIM
    o = of_ref[...] + ob_ref[...]
    on = jnp.concatenate([_rms_norm(o[:, h * dh:(h + 1) * dh], onorm_ref[...])
                          for h in range(GDN_HEADS)], axis=1)
    a = on * _silu(z_ref[...])
    cf = _silu(_layer_norm(y_ref[...], clng_ref[...], clnb_ref[...]))
    mix = _bdot(a, w1_ref[...]) + _bdot(cf, w2_ref[...])
    gate = mods_ref[0, 5:6, :]
    o_ref[...] = _layer_norm(alpha * x_ref[...] + gate * mix, lng_ref[...], lnb_ref[...])


def _even_merge(of2, ob2, z2, y2, x2, mods, seq, onorm, clng, clnb, w1, w2, ln_g, ln_b, alpha):
    m, d = x2.shape
    n = of2.shape[1]
    tm = _row_tile(m, seq)
    rpm = seq // tm if mods.shape[0] > 1 else m // tm
    row = lambda w: pl.BlockSpec((tm, w), lambda i: (i, 0))
    return pl.pallas_call(
        functools.partial(_even_merge_kernel, alpha=alpha),
        grid=(m // tm,),
        in_specs=[row(n), row(n), row(n), row(n), row(d),
                  pl.BlockSpec((1, N_MOD, d), lambda i: (i // rpm, 0, 0)),
                  _resident((1, GDN_HEAD_DIM)), _resident((1, n)), _resident((1, n)),
                  _resident(w1.shape), _resident(w2.shape), _resident((1, d)), _resident((1, d))],
        out_specs=row(d),
        out_shape=jax.ShapeDtypeStruct((m, d), F32),
        compiler_params=_params("parallel"),
        name="even_merge_out_proj",
    )(of2, ob2, z2, y2, x2, mods, onorm.reshape(1, -1), clng.reshape(1, n), clnb.reshape(1, n),
      w1, w2, ln_g.reshape(1, d), ln_b.reshape(1, d))


def _attn_proj_kernel(x_ref, mods_ref, w_ref, qn_ref, kn_ref, cos_ref, sin_ref, q_ref, k_ref, v_ref,
                      *, rope):
    hd = ATT_HEAD_DIM
    x = x_ref[...]
    shift = mods_ref[0, 3:4, :]
    scale = mods_ref[0, 4:5, :]
    h = (x * (1.0 + scale) + shift).astype(BF16)
    y = jnp.dot(h, w_ref[...], preferred_element_type=F32)

    def head(t, gain, post):
        t = _rms_norm(t, gain)
        if rope:
            t = t * cos_ref[...] + pltpu.roll(t, hd // 2, axis=1) * sin_ref[...]
        return (t * post).astype(BF16)

    for i in range(ATT_HEADS):
        q_ref[:, i * hd:(i + 1) * hd] = head(y[:, i * hd:(i + 1) * hd], qn_ref[...], hd ** -0.5)
    off = ATT_HEADS * hd
    for i in range(ATT_KV_HEADS):
        k_ref[:, i * hd:(i + 1) * hd] = head(y[:, off + i * hd:off + (i + 1) * hd], kn_ref[...], 1.0)
    off += ATT_KV_HEADS * hd
    v_ref[...] = y[:, off:off + ATT_KV_HEADS * hd].astype(BF16)


def _attn_proj(x2, mods, seq, w, qn, kn, cos, sin, rope):
    m, d = x2.shape
    hd = ATT_HEAD_DIM
    tm = _row_tile(m, seq)
    rpm = seq // tm if mods.shape[0] > 1 else m // tm
    ntab = cos.shape[0] // tm
    row = lambda w_: pl.BlockSpec((tm, w_), lambda i: (i, 0))
    tab = pl.BlockSpec((tm, hd), lambda i: (i % ntab, 0))
    return pl.pallas_call(
        functools.partial(_attn_proj_kernel, rope=rope),
        grid=(m // tm,),
        in_specs=[row(d), pl.BlockSpec((1, N_MOD, d), lambda i: (i // rpm, 0, 0)),
                  _resident(w.shape), _resident((1, hd)), _resident((1, hd)), tab, tab],
        out_specs=[row(ATT_HEADS * hd), row(ATT_KV_HEADS * hd), row(ATT_KV_HEADS * hd)],
        out_shape=[jax.ShapeDtypeStruct((m, ATT_HEADS * hd), BF16),
                   jax.ShapeDtypeStruct((m, ATT_KV_HEADS * hd), BF16),
                   jax.ShapeDtypeStruct((m, ATT_KV_HEADS * hd), BF16)],
        compiler_params=_params("parallel"),
        name="attn_in_proj",
    )(x2, mods, w, qn.reshape(1, hd), kn.reshape(1, hd), cos, sin)


def _attn_kernel(q_ref, kl_ref, vl_ref, kc_ref, vc_ref, o_ref):
    hd = ATT_HEAD_DIM
    kl, vl, kc, vc = kl_ref[0], vl_ref[0], kc_ref[0], vc_ref[0]
    for g in range(ATT_GROUP):
        q = q_ref[0, :, g * hd:(g + 1) * hd]
        sl = _bdot_nt(q, kl)
        sc = _bdot_nt(q, kc)
        mx = jnp.maximum(jnp.max(sl, axis=-1, keepdims=True), jnp.max(sc, axis=-1, keepdims=True))
        p_l = jnp.exp(sl - mx)
        p_c = jnp.exp(sc - mx)
        den = jnp.sum(p_l, axis=-1, keepdims=True) + jnp.sum(p_c, axis=-1, keepdims=True)
        o = _bdot(p_l, vl) + _bdot(p_c, vc)
        o_ref[0, :, g * hd:(g + 1) * hd] = (o / den).astype(BF16)


def _attention(q3, kl3, vl3, kc3, vc3):
    b, seq, _ = q3.shape
    ct = kc3.shape[1]
    hd = ATT_HEAD_DIM
    tq = min(ATT_Q_TILE, seq)
    gw = ATT_GROUP * hd
    kv = lambda n: pl.BlockSpec((1, n, hd), lambda i, g, t: (i, 0, g))
    return pl.pallas_call(
        _attn_kernel,
        grid=(b, ATT_KV_HEADS, seq // tq),
        in_specs=[pl.BlockSpec((1, tq, gw), lambda i, g, t: (i, t, g)),
                  kv(seq), kv(seq), kv(ct), kv(ct)],
        out_specs=pl.BlockSpec((1, tq, gw), lambda i, g, t: (i, t, g)),
        out_shape=jax.ShapeDtypeStruct((b, seq, ATT_HEADS * hd), BF16),
        compiler_params=_params("parallel", "parallel", "arbitrary"),
        name="gqa_attention",
    )(q3, kl3, vl3, kc3, vc3)


def _out_proj_kernel(a_ref, x_ref, mods_ref, w_ref, lng_ref, lnb_ref, o_ref, *, alpha):
    mix = jnp.dot(a_ref[...], w_ref[...], preferred_element_type=F32)
    gate = mods_ref[0, 5:6, :]
    o_ref[...] = _layer_norm(alpha * x_ref[...] + gate * mix, lng_ref[...], lnb_ref[...])


def _out_proj(a2, x2, mods, seq, w, ln_g, ln_b, alpha):
    m, d = x2.shape
    n = a2.shape[1]
    tm = _row_tile(m, seq)
    rpm = seq // tm if mods.shape[0] > 1 else m // tm
    row = lambda w_: pl.BlockSpec((tm, w_), lambda i: (i, 0))
    return pl.pallas_call(
        functools.partial(_out_proj_kernel, alpha=alpha),
        grid=(m // tm,),
        in_specs=[row(n), row(d), pl.BlockSpec((1, N_MOD, d), lambda i: (i // rpm, 0, 0)),
                  _resident(w.shape), _resident((1, d)), _resident((1, d))],
        out_specs=row(d),
        out_shape=jax.ShapeDtypeStruct((m, d), F32),
        compiler_params=_params("parallel"),
        name="attn_out_proj",
    )(a2, x2, mods, w, ln_g.reshape(1, d), ln_b.reshape(1, d))


def _rope_tables(seq):
    rows = seq // GRID_W
    row = jnp.repeat(jnp.arange(rows, dtype=F32), GRID_W)
    col = jnp.tile(jnp.arange(GRID_W, dtype=F32), rows)
    n_freq = ATT_HEAD_DIM // 4
    inv_freq = jnp.float32(ROPE_THETA) ** (-jnp.arange(n_freq, dtype=F32) / n_freq)
    ang = jnp.concatenate([row[:, None] * inv_freq, col[:, None] * inv_freq], axis=-1)
    cos, sin = jnp.cos(ang), jnp.sin(ang)
    return jnp.concatenate([cos, cos], axis=-1), jnp.concatenate([-sin, sin], axis=-1)


def _split_pairs(n_heads):
    hd = ATT_HEAD_DIM
    within = jnp.concatenate([jnp.arange(0, hd, 2), jnp.arange(1, hd, 2)])
    return (jnp.arange(n_heads)[:, None] * hd + within[None, :]).reshape(-1)


def _lane_row(v):
    flat = v.astype(F32).reshape(-1)
    return jnp.zeros((1, V7X_LANES), F32).at[0, :flat.shape[0]].set(flat)


def kernel(x, c, ctx, c_ctx, ada_w, ada_b, ln_g, ln_b, ffn_w_gu, ffn_w_down, even_w_in, even_qkv_conv,
           gdn_a_log, gdn_dt_bias, gdn_out_norm, cf_dw_conv, cf_dw_bias, cf_ln_g, cf_ln_b, even_w_out,
           attn_w_in, attn_q_norm, attn_k_norm, attn_w_out):
    b, seq, d = x.shape
    ct = ctx.shape[1]
    depth = ada_w.shape[0]
    alpha = (2.0 * depth) ** 0.25
    nh, dh = GDN_HEADS, GDN_HEAD_DIM
    gw = nh * dh
    assert b + 1 <= COND_ROWS and depth <= 2

    cond = jnp.zeros((COND_ROWS, d), F32).at[:b].set(c).at[b].set(c_ctx)
    mods = _ada(cond, ada_w, ada_b)
    xl = x.reshape(b * seq, d)
    xc = ctx.reshape(b * ct, d)

    for layer in range(depth):
        last = layer == depth - 1
        i = layer // 2
        ml = mods[layer, :b].reshape(b, N_MOD, d)
        mc = mods[layer, b:b + 1].reshape(1, N_MOD, d)
        wgu = ffn_w_gu[layer].astype(BF16)
        wdn = ffn_w_down[layer].astype(BF16)
        lg, lb = ln_g[layer], ln_b[layer]

        xl = _ffn(xl, ml, seq, wgu[0], wdn[0], lg[0], lb[0], 0, alpha)
        xc = _ffn(xc, mc, ct, wgu[0], wdn[0], lg[0], lb[0], 0, alpha)

        if layer % 2 == 0:
            w_in = even_w_in[i]
            n_ab = 4 * nh
            n_cf = (w_in.shape[1] - 4 * gw - n_ab) // 2
            w_pad = jnp.concatenate(
                [w_in[:, :4 * gw],
                 jnp.pad(w_in[:, 4 * gw:4 * gw + n_ab], ((0, 0), (0, V7X_LANES - n_ab))),
                 w_in[:, 4 * gw + n_ab:]], axis=1).astype(BF16)
            alog_row, dtb_row = _lane_row(gdn_a_log[i]), _lane_row(gdn_dt_bias[i])
            w_out = even_w_out[i].astype(BF16)
            zero_b = jnp.zeros((3 * gw,), F32)

            def mixer_in(x2, m_, n_seq):
                qkv, z, gb, u = _even_proj(x2, m_, n_seq, w_pad, alog_row, dtb_row, 3 * gw, gw, n_cf)
                qkv = _dw_conv(qkv.reshape(b, n_seq, 3 * gw), even_qkv_conv[i], zero_b, True)
                y = _dw_conv(u.reshape(b, n_seq, n_cf), cf_dw_conv[i], cf_dw_bias[i], False)
                return qkv, z, gb.reshape(b, n_seq, V7X_LANES), y.reshape(b * n_seq, n_cf)

            def mixer_out(of, ob, z, y, x2, m_, n_seq):
                return _even_merge(of.reshape(-1, gw), ob.reshape(-1, gw), z, y, x2, m_, n_seq,
                                   gdn_out_norm[i], cf_ln_g[i], cf_ln_b[i], w_out[:gw], w_out[gw:],
                                   lg[1], lb[1], alpha)

            qkv_c, z_c, gb_c, y_c = mixer_in(xc, mc, ct)
            qkv_l, z_l, gb_l, y_l = mixer_in(xl, ml, seq)
            of_c, ob_c, s_ctx = _gdn_scan(qkv_c, gb_c, jnp.zeros((b, 2 * nh, dh, dh), F32))
            of_l, ob_l, _ = _gdn_scan(qkv_l, gb_l, s_ctx)
            xl = mixer_out(of_l, ob_l, z_l, y_l, xl, ml, seq)
            if not last:
                xc = mixer_out(of_c, ob_c, z_c, y_c, xc, mc, ct)
        else:
            assert last, "context attention output is only needed before a further layer"
            hd = ATT_HEAD_DIM
            nq, nkv = ATT_HEADS * hd, ATT_KV_HEADS * hd
            w_in = attn_w_in[i]
            w_perm = jnp.concatenate([w_in[:, :nq][:, _split_pairs(ATT_HEADS)],
                                      w_in[:, nq:nq + nkv][:, _split_pairs(ATT_KV_HEADS)],
                                      w_in[:, nq + nkv:]], axis=1).astype(BF16)
            within = _split_pairs(1)
            qn, kn = attn_q_norm[i][within], attn_k_norm[i][within]
            cos, sin = _rope_tables(seq)
            q_l, k_l, v_l = _attn_proj(xl, ml, seq, w_perm, qn, kn, cos, sin, True)
            _, k_c, v_c = _attn_proj(xc, mc, ct, w_perm, qn, kn, cos, sin, False)
            o = _attention(q_l.reshape(b, seq, nq), k_l.reshape(b, seq, nkv), v_l.reshape(b, seq, nkv),
                           k_c.reshape(b, ct, nkv), v_c.reshape(b, ct, nkv))
            xl = _out_proj(o.reshape(b * seq, nq), xl, ml, seq, attn_w_out[i].astype(BF16),
                           lg[1], lb[1], alpha)

        xl = _ffn(xl, ml, seq, wgu[1], wdn[1], lg[2], lb[2], 6, alpha)
        if not last:
            xc = _ffn(xc, mc, ct, wgu[1], wdn[1], lg[2], lb[2], 6, alpha)

    return xl.reshape(b, seq, d)
```

```python
import functools

import jax
import jax.numpy as jnp
from jax import lax
from jax.experimental import pallas as pl
from jax.experimental.pallas import tpu as pltpu

F32 = jnp.float32
BF16 = jnp.bfloat16

GRID_W = 64
LN_EPS = 1e-5
RMS_EPS = 1e-6
N_MOD = 9
MACARON_WEIGHT = 0.5
GDN_HEADS = 4
GDN_HEAD_DIM = 128
GDN_CHUNK = 64
ATT_HEADS = 8
ATT_KV_HEADS = 2
ATT_GROUP = ATT_HEADS // ATT_KV_HEADS
ATT_HEAD_DIM = 128
ROPE_THETA = 10000.0

V7X_LANES = 128
V7X_SUBLANES = 8
V7X_VMEM_LIMIT_BYTES = 56 * 1024 * 1024

COND_ROWS = 16
CONV_PAD = 16
ROW_TILE = 512
CONV_ROWS = 256
GDN_TILE = 512
ATT_Q_TILE = 256
ATT_K_TILE = 512
LOG2_E = 1.4426950408889634


def _sigmoid(x):
    return 1.0 / (1.0 + jnp.exp(-x))


def _silu(x):
    return x * _sigmoid(x)


def _softplus(x):
    return jnp.maximum(x, 0.0) + jnp.log1p(jnp.exp(-jnp.abs(x)))


def _layer_norm(x, g, b):
    mu = jnp.mean(x, axis=-1, keepdims=True)
    xc = x - mu
    var = jnp.mean(xc * xc, axis=-1, keepdims=True)
    return xc * lax.rsqrt(var + LN_EPS) * g + b


def _rms_norm(x, g):
    return x * lax.rsqrt(jnp.mean(x * x, axis=-1, keepdims=True) + RMS_EPS) * g


def _bdot(a, b):
    return jnp.dot(a.astype(BF16), b.astype(BF16), preferred_element_type=F32)


def _bdot_nt(a, b):
    return lax.dot_general(a.astype(BF16), b.astype(BF16), (((1,), (1,)), ((), ())),
                           preferred_element_type=F32)


def _params(*semantics):
    return pltpu.CompilerParams(dimension_semantics=semantics,
                                vmem_limit_bytes=V7X_VMEM_LIMIT_BYTES)


def _resident(shape):
    nd = len(shape)
    return pl.BlockSpec(shape, lambda *_: (0,) * nd, pipeline_mode=pl.Buffered(1))


def _row_tile(m, seq):
    t = min(ROW_TILE, seq)
    assert m % t == 0 and seq % t == 0
    return t


def _ada_kernel(c_ref, w_ref, b_ref, o_ref):
    c = c_ref[...]
    o_ref[0] = _bdot(_silu(c), w_ref[0]) + b_ref[0]


def _ada(cond, ada_w, ada_b):
    depth, d, nd = ada_w.shape
    return pl.pallas_call(
        _ada_kernel,
        grid=(depth, nd // d),
        in_specs=[pl.BlockSpec((COND_ROWS, d), lambda l, j: (0, 0)),
                  pl.BlockSpec((1, d, d), lambda l, j: (l, 0, j)),
                  pl.BlockSpec((1, 1, d), lambda l, j: (l, 0, j))],
        out_specs=pl.BlockSpec((1, COND_ROWS, d), lambda l, j: (l, 0, j)),
        out_shape=jax.ShapeDtypeStruct((depth, COND_ROWS, nd), F32),
        compiler_params=_params("parallel", "parallel"),
        name="ada_modulation",
    )(cond, ada_w, ada_b.reshape(depth, 1, nd))


def _ffn_kernel(x_ref, mods_ref, wgu_ref, wd_ref, lng_ref, lnb_ref, o_ref, *, mod_base, d_ff, alpha):
    x = x_ref[...]
    shift = mods_ref[0, mod_base:mod_base + 1, :]
    scale = mods_ref[0, mod_base + 1:mod_base + 2, :]
    gate = mods_ref[0, mod_base + 2:mod_base + 3, :]
    h = (x * (1.0 + scale) + shift).astype(BF16)
    g = jnp.dot(h, wgu_ref[:, :d_ff], preferred_element_type=F32)
    u = jnp.dot(h, wgu_ref[:, d_ff:], preferred_element_type=F32)
    a = (_silu(g) * u).astype(BF16)
    y = jnp.dot(a, wd_ref[...], preferred_element_type=F32)
    o_ref[...] = _layer_norm(alpha * x + (MACARON_WEIGHT * gate) * y, lng_ref[...], lnb_ref[...])


def _ffn(x2, mods, seq, w_gu, w_down, ln_g, ln_b, mod_base, alpha):
    m, d = x2.shape
    d_ff = w_down.shape[0]
    tm = _row_tile(m, seq)
    rpm = seq // tm if mods.shape[0] > 1 else m // tm
    return pl.pallas_call(
        functools.partial(_ffn_kernel, mod_base=mod_base, d_ff=d_ff, alpha=alpha),
        grid=(m // tm,),
        in_specs=[pl.BlockSpec((tm, d), lambda i: (i, 0)),
                  pl.BlockSpec((1, N_MOD, d), lambda i: (i // rpm, 0, 0)),
                  _resident((d, 2 * d_ff)), _resident((d_ff, d)),
                  _resident((1, d)), _resident((1, d))],
        out_specs=pl.BlockSpec((tm, d), lambda i: (i, 0)),
        out_shape=jax.ShapeDtypeStruct((m, d), F32),
        compiler_params=_params("parallel"),
        name="ffn_sublayer",
    )(x2, mods, w_gu, w_down, ln_g.reshape(1, d), ln_b.reshape(1, d))


def _even_proj_kernel(x_ref, mods_ref, w_ref, alog_ref, dtb_ref, qkv_ref, z_ref, gb_ref, u_ref,
                      *, n_qkv, n_z, n_cf):
    x = x_ref[...]
    shift = mods_ref[0, 3:4, :]
    scale = mods_ref[0, 4:5, :]
    h = (x * (1.0 + scale) + shift).astype(BF16)
    y = jnp.dot(h, w_ref[...], preferred_element_type=F32)
    qkv_ref[...] = y[:, :n_qkv]
    o = n_qkv
    z_ref[...] = y[:, o:o + n_z]
    o += n_z
    ab = y[:, o:o + V7X_LANES]
    o += V7X_LANES
    lane = lax.broadcasted_iota(jnp.int32, ab.shape, 1)
    g = -jnp.exp(alog_ref[...]) * _softplus(ab + dtb_ref[...])
    gb_ref[...] = jnp.where(lane < 2 * GDN_HEADS, g, _sigmoid(ab))
    val = y[:, o:o + n_cf]
    gate = y[:, o + n_cf:o + 2 * n_cf]
    u_ref[...] = val * _sigmoid(gate)


def _even_proj(x2, mods, seq, w_pad, alog_row, dtb_row, n_qkv, n_z, n_cf):
    m, d = x2.shape
    tm = _row_tile(m, seq)
    rpm = seq // tm if mods.shape[0] > 1 else m // tm
    row = lambda w: pl.BlockSpec((tm, w), lambda i: (i, 0))
    return pl.pallas_call(
        functools.partial(_even_proj_kernel, n_qkv=n_qkv, n_z=n_z, n_cf=n_cf),
        grid=(m // tm,),
        in_specs=[row(d), pl.BlockSpec((1, N_MOD, d), lambda i: (i // rpm, 0, 0)),
                  _resident(w_pad.shape), _resident((1, V7X_LANES)), _resident((1, V7X_LANES))],
        out_specs=[row(n_qkv), row(n_z), row(V7X_LANES), row(n_cf)],
        out_shape=[jax.ShapeDtypeStruct((m, n_qkv), F32), jax.ShapeDtypeStruct((m, n_z), F32),
                   jax.ShapeDtypeStruct((m, V7X_LANES), F32), jax.ShapeDtypeStruct((m, n_cf), F32)],
        compiler_params=_params("parallel"),
        name="even_in_proj",
    )(x2, mods, w_pad, alog_row, dtb_row)


def _conv_kernel(x_ref, w_ref, b_ref, o_ref, pad_ref, *, taps, seq, rows, qk_norm, n_q, n_qk):
    half = taps // 2
    zeros = jnp.zeros((CONV_PAD, V7X_LANES), F32)
    pad_ref[0:CONV_PAD, :] = zeros
    pad_ref[CONV_PAD + seq:CONV_PAD + seq + CONV_PAD, :] = zeros
    pad_ref[CONV_PAD:CONV_PAD + seq, :] = x_ref[0]
    j = pl.program_id(1)
    for c in range(seq // rows):
        base = CONV_PAD + c * rows - half
        acc = w_ref[0:1, :] * pad_ref[base:base + rows, :]
        for k in range(1, taps):
            acc = acc + w_ref[k:k + 1, :] * pad_ref[base + k:base + k + rows, :]
        if qk_norm:
            y = _silu(acc)
            inv = lax.rsqrt(jnp.sum(y * y, axis=-1, keepdims=True) + RMS_EPS)
            fac = jnp.where(j < n_q, inv * (GDN_HEAD_DIM ** -0.5), jnp.where(j < n_qk, inv, 1.0))
            out = y * fac
        else:
            out = acc + b_ref[...]
        o_ref[0, c * rows:(c + 1) * rows, :] = out


def _dw_conv(x3, w, bias, qk_norm):
    b, seq, ch = x3.shape
    taps = w.shape[0]
    assert taps // 2 <= CONV_PAD and ch % V7X_LANES == 0
    rows = min(CONV_ROWS, seq)
    return pl.pallas_call(
        functools.partial(_conv_kernel, taps=taps, seq=seq, rows=rows, qk_norm=qk_norm,
                          n_q=GDN_HEADS, n_qk=2 * GDN_HEADS),
        grid=(b, ch // V7X_LANES),
        in_specs=[pl.BlockSpec((1, seq, V7X_LANES), lambda i, j: (i, 0, j)),
                  pl.BlockSpec((taps, V7X_LANES), lambda i, j: (0, j)),
                  pl.BlockSpec((1, V7X_LANES), lambda i, j: (0, j))],
        out_specs=pl.BlockSpec((1, seq, V7X_LANES), lambda i, j: (i, 0, j)),
        out_shape=jax.ShapeDtypeStruct((b, seq, ch), F32),
        scratch_shapes=[pltpu.VMEM((seq + 2 * CONV_PAD, V7X_LANES), F32)],
        compiler_params=_params("parallel", "parallel"),
        name="qkv_conv" if qk_norm else "conformer_conv",
    )(x3, w, bias.reshape(1, ch))


def _gdn_kernel(qkvf_ref, qkvb_ref, gbf_ref, gbb_ref, s0_ref, of_ref, ob_ref, sout_ref, s_ref,
                *, n_chunks):
    nh, dh, ck = GDN_HEADS, GDN_HEAD_DIM, GDN_CHUNK
    width = nh * dh
    t = pl.program_id(1)

    @pl.when(t == 0)
    def _():
        s_ref[...] = s0_ref[0]

    row = lax.broadcasted_iota(jnp.int32, (ck, ck), 0)
    col = lax.broadcasted_iota(jnp.int32, (ck, ck), 1)
    eye = (row == col).astype(F32)
    masks = ((row >= col, row > col), (row <= col, row < col))
    n_double = ck.bit_length() - 2

    dirs = ((qkvf_ref, gbf_ref, of_ref), (qkvb_ref, gbb_ref, ob_ref))
    chains = [(d, h) for d in range(2) for h in range(nh)]

    def chunk_step(ci, carry):
        r0s, gbcs, gcs, gcts = [], [], [], []
        for d, (_, gb_ref, _) in enumerate(dirs):
            r0 = pl.multiple_of((ci if d == 0 else n_chunks - 1 - ci) * ck, ck)
            gbc = gb_ref[0, pl.ds(r0, ck), :]
            gc_all = jnp.dot(masks[d][0].astype(F32), gbc, precision=lax.Precision.HIGHEST,
                             preferred_element_type=F32)
            r0s.append(r0)
            gbcs.append(gbc)
            gcs.append(gc_all)
            gcts.append(gc_all.T)

        st = []
        for d, h in chains:
            lane = d * nh + h
            qkv_ref = dirs[d][0]
            mask = masks[d][0]
            gcol = gcs[d][:, lane:lane + 1]
            grow = gcts[d][lane:lane + 1, :]
            beta = gbcs[d][:, 2 * nh + lane:2 * nh + lane + 1]
            q = qkv_ref[0, pl.ds(r0s[d], ck), h * dh:(h + 1) * dh]
            k = qkv_ref[0, pl.ds(r0s[d], ck), width + h * dh:width + (h + 1) * dh]
            v = qkv_ref[0, pl.ds(r0s[d], ck), 2 * width + h * dh:2 * width + (h + 1) * dh]
            decay = jnp.where(mask, jnp.exp(jnp.where(mask, gcol - grow, 0.0)), 0.0)
            kb = k * beta
            eg = jnp.exp(gcol)
            last = ck - 1 if d == 0 else 0
            glast = gcol[last:last + 1, :]
            st.append(dict(lane=lane, q=q, k=k, decay=decay, kb=kb, eg=eg, glast=glast, gcol=gcol,
                           rhs=jnp.concatenate([v * beta, kb * eg], axis=1)))
        for c in st:
            c["kkqk"] = _bdot_nt(jnp.concatenate([c["kb"], c["q"]], axis=0), c["k"])
        for c, (d, h) in zip(st, chains):
            c["mp"] = jnp.where(masks[d][1], c["kkqk"][:ck] * c["decay"], 0.0)
            c["attn"] = c["kkqk"][ck:] * c["decay"]
            c["inv"] = eye - c["mp"]
        for _ in range(n_double):
            for c in st:
                c["mp"] = _bdot(c["mp"], c["mp"])
            for c in st:
                c["inv"] = c["inv"] + _bdot(c["inv"], c["mp"])
        for c in st:
            c["uw"] = _bdot(c["inv"], c["rhs"])
        for c in st:
            c["s"] = s_ref[c["lane"]]
            c["wq_s"] = _bdot(jnp.concatenate([c["uw"][:, dh:], c["q"] * c["eg"]], axis=0), c["s"])
        for c in st:
            c["v_new"] = c["uw"][:, :dh] - c["wq_s"][:ck]
            c["o"] = c["wq_s"][ck:] + _bdot(c["attn"], c["v_new"])
        for c in st:
            kt = c["k"] * jnp.exp(c["glast"] - c["gcol"])
            s_ref[c["lane"]] = c["s"] * jnp.exp(c["glast"]) + _bdot(kt.T, c["v_new"])
        for c, (d, h) in zip(st, chains):
            dirs[d][2][0, pl.ds(r0s[d], ck), h * dh:(h + 1) * dh] = c["o"]
        return carry

    lax.fori_loop(0, n_chunks, chunk_step, 0)

    @pl.when(t == pl.num_programs(1) - 1)
    def _():
        sout_ref[0] = s_ref[...]


def _gdn_scan(qkv3, gb3, s0):
    b, seq, w3 = qkv3.shape
    nh, dh = GDN_HEADS, GDN_HEAD_DIM
    lt = min(GDN_TILE, seq)
    assert seq % lt == 0 and lt % GDN_CHUNK == 0
    nt = seq // lt
    fwd = lambda i, t: (i, t, 0)
    bwd = lambda i, t: (i, nt - 1 - t, 0)
    st = pl.BlockSpec((1, 2 * nh, dh, dh), lambda i, t: (i, 0, 0, 0))
    return pl.pallas_call(
        functools.partial(_gdn_kernel, n_chunks=lt // GDN_CHUNK),
        grid=(b, nt),
        in_specs=[pl.BlockSpec((1, lt, w3), fwd), pl.BlockSpec((1, lt, w3), bwd),
                  pl.BlockSpec((1, lt, V7X_LANES), fwd), pl.BlockSpec((1, lt, V7X_LANES), bwd), st],
        out_specs=[pl.BlockSpec((1, lt, nh * dh), fwd), pl.BlockSpec((1, lt, nh * dh), bwd), st],
        out_shape=[jax.ShapeDtypeStruct((b, seq, nh * dh), F32),
                   jax.ShapeDtypeStruct((b, seq, nh * dh), F32),
                   jax.ShapeDtypeStruct((b, 2 * nh, dh, dh), F32)],
        scratch_shapes=[pltpu.VMEM((2 * nh, dh, dh), F32)],
        compiler_params=_params("parallel", "arbitrary"),
        name="gdn_scan",
    )(qkv3, qkv3, gb3, gb3, s0)


def _even_merge_kernel(of_ref, ob_ref, z_ref, y_ref, x_ref, mods_ref, onorm_ref, clng_ref, clnb_ref,
                       w1_ref, w2_ref, lng_ref, lnb_ref, o_ref, *, alpha):
    dh = GDN_HEAD_DIM
    o = of_ref[...] + ob_ref[...]
    on = jnp.concatenate([_rms_norm(o[:, h * dh:(h + 1) * dh], onorm_ref[...])
                          for h in range(GDN_HEADS)], axis=1)
    a = on * _silu(z_ref[...])
    cf = _silu(_layer_norm(y_ref[...], clng_ref[...], clnb_ref[...]))
    mix = _bdot(a, w1_ref[...]) + _bdot(cf, w2_ref[...])
    gate = mods_ref[0, 5:6, :]
    o_ref[...] = _layer_norm(alpha * x_ref[...] + gate * mix, lng_ref[...], lnb_ref[...])


def _even_merge(of2, ob2, z2, y2, x2, mods, seq, onorm, clng, clnb, w1, w2, ln_g, ln_b, alpha):
    m, d = x2.shape
    n = of2.shape[1]
    tm = _row_tile(m, seq)
    rpm = seq // tm if mods.shape[0] > 1 else m // tm
    row = lambda w: pl.BlockSpec((tm, w), lambda i: (i, 0))
    return pl.pallas_call(
        functools.partial(_even_merge_kernel, alpha=alpha),
        grid=(m // tm,),
        in_specs=[row(n), row(n), row(n), row(n), row(d),
                  pl.BlockSpec((1, N_MOD, d), lambda i: (i // rpm, 0, 0)),
                  _resident((1, GDN_HEAD_DIM)), _resident((1, n)), _resident((1, n)),
                  _resident(w1.shape), _resident(w2.shape), _resident((1, d)), _resident((1, d))],
        out_specs=row(d),
        out_shape=jax.ShapeDtypeStruct((m, d), F32),
        compiler_params=_params("parallel"),
        name="even_merge_out_proj",
    )(of2, ob2, z2, y2, x2, mods, onorm.reshape(1, -1), clng.reshape(1, n), clnb.reshape(1, n),
      w1, w2, ln_g.reshape(1, d), ln_b.reshape(1, d))


def _attn_proj_kernel(x_ref, mods_ref, w_ref, qn_ref, kn_ref, cos_ref, sin_ref, q_ref, k_ref, v_ref,
                      *, rope):
    hd = ATT_HEAD_DIM
    x = x_ref[...]
    shift = mods_ref[0, 3:4, :]
    scale = mods_ref[0, 4:5, :]
    h = (x * (1.0 + scale) + shift).astype(BF16)
    y = jnp.dot(h, w_ref[...], preferred_element_type=F32)

    def head(t, gain, post):
        t = _rms_norm(t, gain)
        if rope:
            t = t * cos_ref[...] + pltpu.roll(t, hd // 2, axis=1) * sin_ref[...]
        return (t * post).astype(BF16)

    for i in range(ATT_HEADS):
        q_ref[:, i * hd:(i + 1) * hd] = head(y[:, i * hd:(i + 1) * hd], qn_ref[...],
                                             hd ** -0.5 * LOG2_E)
    off = ATT_HEADS * hd
    for i in range(ATT_KV_HEADS):
        k_ref[:, i * hd:(i + 1) * hd] = head(y[:, off + i * hd:off + (i + 1) * hd], kn_ref[...], 1.0)
    off += ATT_KV_HEADS * hd
    v_ref[...] = y[:, off:off + ATT_KV_HEADS * hd].astype(BF16)


def _attn_proj(x2, mods, seq, w, qn, kn, cos, sin, rope):
    m, d = x2.shape
    hd = ATT_HEAD_DIM
    tm = _row_tile(m, seq)
    rpm = seq // tm if mods.shape[0] > 1 else m // tm
    ntab = cos.shape[0] // tm
    row = lambda w_: pl.BlockSpec((tm, w_), lambda i: (i, 0))
    tab = pl.BlockSpec((tm, hd), lambda i: (i % ntab, 0))
    return pl.pallas_call(
        functools.partial(_attn_proj_kernel, rope=rope),
        grid=(m // tm,),
        in_specs=[row(d), pl.BlockSpec((1, N_MOD, d), lambda i: (i // rpm, 0, 0)),
                  _resident(w.shape), _resident((1, hd)), _resident((1, hd)), tab, tab],
        out_specs=[row(ATT_HEADS * hd), row(ATT_KV_HEADS * hd), row(ATT_KV_HEADS * hd)],
        out_shape=[jax.ShapeDtypeStruct((m, ATT_HEADS * hd), BF16),
                   jax.ShapeDtypeStruct((m, ATT_KV_HEADS * hd), BF16),
                   jax.ShapeDtypeStruct((m, ATT_KV_HEADS * hd), BF16)],
        compiler_params=_params("parallel"),
        name="attn_in_proj",
    )(x2, mods, w, qn.reshape(1, hd), kn.reshape(1, hd), cos, sin)


def _attn_kernel(q_ref, kl_ref, vl_ref, kc_ref, vc_ref, o_ref, s_ref, vx_ref, *, tk):
    hd = ATT_HEAD_DIM
    seq, ct = kl_ref.shape[1], kc_ref.shape[1]
    tq = q_ref.shape[1]

    @pl.when(pl.program_id(2) == 0)
    def _():
        vx_ref[0:seq, 0:hd] = vl_ref[0]
        vx_ref[seq:seq + ct, 0:hd] = vc_ref[0]
        vx_ref[:, hd:2 * hd] = jnp.ones((seq + ct, hd), BF16)

    chunks = [(kl_ref, j * tk, min(tk, seq - j * tk), j * tk) for j in range(pl.cdiv(seq, tk))]
    chunks += [(kc_ref, j * tk, min(tk, ct - j * tk), seq + j * tk) for j in range(pl.cdiv(ct, tk))]
    for g in range(ATT_GROUP):
        q = q_ref[0, :, g * hd:(g + 1) * hd]
        buf = g % 2
        mrun = jnp.full((tq, V7X_LANES), -jnp.inf, F32)
        for k_ref, start, size, off in chunks:
            s = _bdot_nt(q, k_ref[0, start:start + size, :])
            s_ref[buf, :, off:off + size] = s
            for c in range(size // V7X_LANES):
                mrun = jnp.maximum(mrun, s[:, c * V7X_LANES:(c + 1) * V7X_LANES])
        mx = jnp.max(mrun, axis=-1, keepdims=True)
        acc = jnp.zeros((tq, 2 * hd), F32)
        for _, _, size, off in chunks:
            p = jnp.exp2(s_ref[buf, :, off:off + size] - mx).astype(BF16)
            acc = acc + jnp.dot(p, vx_ref[off:off + size, :], preferred_element_type=F32)
        o_ref[0, :, g * hd:(g + 1) * hd] = (acc[:, :hd] / acc[:, hd:hd + 1]).astype(BF16)


def _attention(q3, kl3, vl3, kc3, vc3):
    b, seq, _ = q3.shape
    ct = kc3.shape[1]
    hd = ATT_HEAD_DIM
    tq = min(ATT_Q_TILE, seq)
    tk = min(ATT_K_TILE, seq)
    assert seq % V7X_LANES == 0 and ct % V7X_LANES == 0
    gw = ATT_GROUP * hd
    kv = lambda n: pl.BlockSpec((1, n, hd), lambda i, g, t: (i, 0, g))
    return pl.pallas_call(
        functools.partial(_attn_kernel, tk=tk),
        grid=(b, ATT_KV_HEADS, seq // tq),
        in_specs=[pl.BlockSpec((1, tq, gw), lambda i, g, t: (i, t, g)),
                  kv(seq), kv(seq), kv(ct), kv(ct)],
        out_specs=pl.BlockSpec((1, tq, gw), lambda i, g, t: (i, t, g)),
        out_shape=jax.ShapeDtypeStruct((b, seq, ATT_HEADS * hd), BF16),
        scratch_shapes=[pltpu.VMEM((2, tq, seq + ct), F32), pltpu.VMEM((seq + ct, 2 * hd), BF16)],
        compiler_params=_params("parallel", "parallel", "arbitrary"),
        name="gqa_attention",
    )(q3, kl3, vl3, kc3, vc3)


def _out_proj_kernel(a_ref, x_ref, mods_ref, w_ref, lng_ref, lnb_ref, o_ref, *, alpha):
    mix = jnp.dot(a_ref[...], w_ref[...], preferred_element_type=F32)
    gate = mods_ref[0, 5:6, :]
    o_ref[...] = _layer_norm(alpha * x_ref[...] + gate * mix, lng_ref[...], lnb_ref[...])


def _out_proj(a2, x2, mods, seq, w, ln_g, ln_b, alpha):
    m, d = x2.shape
    n = a2.shape[1]
    tm = _row_tile(m, seq)
    rpm = seq // tm if mods.shape[0] > 1 else m // tm
    row = lambda w_: pl.BlockSpec((tm, w_), lambda i: (i, 0))
    return pl.pallas_call(
        functools.partial(_out_proj_kernel, alpha=alpha),
        grid=(m // tm,),
        in_specs=[row(n), row(d), pl.BlockSpec((1, N_MOD, d), lambda i: (i // rpm, 0, 0)),
                  _resident(w.shape), _resident((1, d)), _resident((1, d))],
        out_specs=row(d),
        out_shape=jax.ShapeDtypeStruct((m, d), F32),
        compiler_params=_params("parallel"),
        name="attn_out_proj",
    )(a2, x2, mods, w, ln_g.reshape(1, d), ln_b.reshape(1, d))


def _rope_tables(seq):
    rows = seq // GRID_W
    row = jnp.repeat(jnp.arange(rows, dtype=F32), GRID_W)
    col = jnp.tile(jnp.arange(GRID_W, dtype=F32), rows)
    n_freq = ATT_HEAD_DIM // 4
    inv_freq = jnp.float32(ROPE_THETA) ** (-jnp.arange(n_freq, dtype=F32) / n_freq)
    ang = jnp.concatenate([row[:, None] * inv_freq, col[:, None] * inv_freq], axis=-1)
    cos, sin = jnp.cos(ang), jnp.sin(ang)
    return jnp.concatenate([cos, cos], axis=-1), jnp.concatenate([-sin, sin], axis=-1)


def _split_pairs(n_heads):
    hd = ATT_HEAD_DIM
    within = jnp.concatenate([jnp.arange(0, hd, 2), jnp.arange(1, hd, 2)])
    return (jnp.arange(n_heads)[:, None] * hd + within[None, :]).reshape(-1)


def _lane_row(v):
    flat = v.astype(F32).reshape(-1)
    return jnp.zeros((1, V7X_LANES), F32).at[0, :flat.shape[0]].set(flat)


def kernel(x, c, ctx, c_ctx, ada_w, ada_b, ln_g, ln_b, ffn_w_gu, ffn_w_down, even_w_in, even_qkv_conv,
           gdn_a_log, gdn_dt_bias, gdn_out_norm, cf_dw_conv, cf_dw_bias, cf_ln_g, cf_ln_b, even_w_out,
           attn_w_in, attn_q_norm, attn_k_norm, attn_w_out):
    b, seq, d = x.shape
    ct = ctx.shape[1]
    depth = ada_w.shape[0]
    alpha = (2.0 * depth) ** 0.25
    nh, dh = GDN_HEADS, GDN_HEAD_DIM
    gw = nh * dh
    assert b + 1 <= COND_ROWS and depth <= 2

    cond = jnp.zeros((COND_ROWS, d), F32).at[:b].set(c).at[b].set(c_ctx)
    mods = _ada(cond, ada_w, ada_b)
    xl = x.reshape(b * seq, d)
    xc = ctx.reshape(b * ct, d)

    for layer in range(depth):
        last = layer == depth - 1
        i = layer // 2
        ml = mods[layer, :b].reshape(b, N_MOD, d)
        mc = mods[layer, b:b + 1].reshape(1, N_MOD, d)
        wgu = ffn_w_gu[layer].astype(BF16)
        wdn = ffn_w_down[layer].astype(BF16)
        lg, lb = ln_g[layer], ln_b[layer]

        xl = _ffn(xl, ml, seq, wgu[0], wdn[0], lg[0], lb[0], 0, alpha)
        xc = _ffn(xc, mc, ct, wgu[0], wdn[0], lg[0], lb[0], 0, alpha)

        if layer % 2 == 0:
            w_in = even_w_in[i]
            n_ab = 4 * nh
            n_cf = (w_in.shape[1] - 4 * gw - n_ab) // 2
            w_pad = jnp.concatenate(
                [w_in[:, :4 * gw],
                 jnp.pad(w_in[:, 4 * gw:4 * gw + n_ab], ((0, 0), (0, V7X_LANES - n_ab))),
                 w_in[:, 4 * gw + n_ab:]], axis=1).astype(BF16)
            alog_row, dtb_row = _lane_row(gdn_a_log[i]), _lane_row(gdn_dt_bias[i])
            w_out = even_w_out[i].astype(BF16)
            zero_b = jnp.zeros((3 * gw,), F32)

            def mixer_in(x2, m_, n_seq):
                qkv, z, gb, u = _even_proj(x2, m_, n_seq, w_pad, alog_row, dtb_row, 3 * gw, gw, n_cf)
                qkv = _dw_conv(qkv.reshape(b, n_seq, 3 * gw), even_qkv_conv[i], zero_b, True)
                y = _dw_conv(u.reshape(b, n_seq, n_cf), cf_dw_conv[i], cf_dw_bias[i], False)
                return qkv, z, gb.reshape(b, n_seq, V7X_LANES), y.reshape(b * n_seq, n_cf)

            def mixer_out(of, ob, z, y, x2, m_, n_seq):
                return _even_merge(of.reshape(-1, gw), ob.reshape(-1, gw), z, y, x2, m_, n_seq,
                                   gdn_out_norm[i], cf_ln_g[i], cf_ln_b[i], w_out[:gw], w_out[gw:],
                                   lg[1], lb[1], alpha)

            qkv_c, z_c, gb_c, y_c = mixer_in(xc, mc, ct)
            qkv_l, z_l, gb_l, y_l = mixer_in(xl, ml, seq)
            of_c, ob_c, s_ctx = _gdn_scan(qkv_c, gb_c, jnp.zeros((b, 2 * nh, dh, dh), F32))
            of_l, ob_l, _ = _gdn_scan(qkv_l, gb_l, s_ctx)
            xl = mixer_out(of_l, ob_l, z_l, y_l, xl, ml, seq)
            if not last:
                xc = mixer_out(of_c, ob_c, z_c, y_c, xc, mc, ct)
        else:
            assert last, "context attention output is only needed before a further layer"
            hd = ATT_HEAD_DIM
            nq, nkv = ATT_HEADS * hd, ATT_KV_HEADS * hd
            w_in = attn_w_in[i]
            w_perm = jnp.concatenate([w_in[:, :nq][:, _split_pairs(ATT_HEADS)],
                                      w_in[:, nq:nq + nkv][:, _split_pairs(ATT_KV_HEADS)],
                                      w_in[:, nq + nkv:]], axis=1).astype(BF16)
            within = _split_pairs(1)
            qn, kn = attn_q_norm[i][within], attn_k_norm[i][within]
            cos, sin = _rope_tables(seq)
            q_l, k_l, v_l = _attn_proj(xl, ml, seq, w_perm, qn, kn, cos, sin, True)
            _, k_c, v_c = _attn_proj(xc, mc, ct, w_perm, qn, kn, cos, sin, False)
            o = _attention(q_l.reshape(b, seq, nq), k_l.reshape(b, seq, nkv), v_l.reshape(b, seq, nkv),
                           k_c.reshape(b, ct, nkv), v_c.reshape(b, ct, nkv))
            xl = _out_proj(o.reshape(b * seq, nq), xl, ml, seq, attn_w_out[i].astype(BF16),
                           lg[1], lb[1], alpha)

        xl = _ffn(xl, ml, seq, wgu[1], wdn[1], lg[2], lb[2], 6, alpha)
        if not last:
            xc = _ffn(xc, mc, ct, wgu[1], wdn[1], lg[2], lb[2], 6, alpha)

    return xl.reshape(b, seq, d)
```

```python
import functools

import jax
import jax.numpy as jnp
from jax import lax
from jax.experimental import pallas as pl
from jax.experimental.pallas import tpu as pltpu

F32 = jnp.float32
BF16 = jnp.bfloat16

GRID_W = 64
LN_EPS = 1e-5
RMS_EPS = 1e-6
N_MOD = 9
MACARON_WEIGHT = 0.5
GDN_HEADS = 4
GDN_HEAD_DIM = 128
GDN_CHUNK = 64
ATT_HEADS = 8
ATT_KV_HEADS = 2
ATT_GROUP = ATT_HEADS // ATT_KV_HEADS
ATT_HEAD_DIM = 128
ROPE_THETA = 10000.0

V7X_LANES = 128
V7X_SUBLANES = 8
V7X_VMEM_LIMIT_BYTES = 56 * 1024 * 1024

COND_ROWS = 16
CONV_PAD = 16
ROW_TILE = 512
CONV_ROWS = 256
GDN_TILE = 512
GDN_BATCH_GROUP = 2
ATT_Q_TILE = 1024
ATT_K_TILE = 256
ATT_Q_TILE_ROWMAX = 256
ATT_K_TILE_ROWMAX = 512
LOG2_E = 1.4426950408889634
ATT_SAFE_SCORE_BOUND = 40.0


def _sigmoid(x):
    return 1.0 / (1.0 + jnp.exp(-x))


def _silu(x):
    return x * _sigmoid(x)


def _softplus(x):
    return jnp.maximum(x, 0.0) + jnp.log1p(jnp.exp(-jnp.abs(x)))


def _layer_norm(x, g, b):
    mu = jnp.mean(x, axis=-1, keepdims=True)
    xc = x - mu
    var = jnp.mean(xc * xc, axis=-1, keepdims=True)
    return xc * lax.rsqrt(var + LN_EPS) * g + b


def _rms_norm(x, g):
    return x * lax.rsqrt(jnp.mean(x * x, axis=-1, keepdims=True) + RMS_EPS) * g


def _bdot(a, b):
    return jnp.dot(a.astype(BF16), b.astype(BF16), preferred_element_type=F32)


def _bdot_nt(a, b):
    return lax.dot_general(a.astype(BF16), b.astype(BF16), (((1,), (1,)), ((), ())),
                           preferred_element_type=F32)


def _params(*semantics):
    return pltpu.CompilerParams(dimension_semantics=semantics,
                                vmem_limit_bytes=V7X_VMEM_LIMIT_BYTES)


def _resident(shape):
    nd = len(shape)
    return pl.BlockSpec(shape, lambda *_: (0,) * nd, pipeline_mode=pl.Buffered(1))


def _row_tile(m, seq):
    t = min(ROW_TILE, seq)
    assert m % t == 0 and seq % t == 0
    return t


def _ada_kernel(c_ref, w_ref, b_ref, o_ref):
    c = c_ref[...]
    o_ref[0] = _bdot(_silu(c), w_ref[0]) + b_ref[0]


def _ada(cond, ada_w, ada_b):
    depth, d, nd = ada_w.shape
    return pl.pallas_call(
        _ada_kernel,
        grid=(depth, nd // d),
        in_specs=[pl.BlockSpec((COND_ROWS, d), lambda l, j: (0, 0)),
                  pl.BlockSpec((1, d, d), lambda l, j: (l, 0, j)),
                  pl.BlockSpec((1, 1, d), lambda l, j: (l, 0, j))],
        out_specs=pl.BlockSpec((1, COND_ROWS, d), lambda l, j: (l, 0, j)),
        out_shape=jax.ShapeDtypeStruct((depth, COND_ROWS, nd), F32),
        compiler_params=_params("parallel", "parallel"),
        name="ada_modulation",
    )(cond, ada_w, ada_b.reshape(depth, 1, nd))


def _ffn_kernel(x_ref, mods_ref, wgu_ref, wd_ref, lng_ref, lnb_ref, o_ref, *, mod_base, d_ff, alpha):
    x = x_ref[...]
    shift = mods_ref[0, mod_base:mod_base + 1, :]
    scale = mods_ref[0, mod_base + 1:mod_base + 2, :]
    gate = mods_ref[0, mod_base + 2:mod_base + 3, :]
    h = (x * (1.0 + scale) + shift).astype(BF16)
    g = jnp.dot(h, wgu_ref[:, :d_ff], preferred_element_type=F32)
    u = jnp.dot(h, wgu_ref[:, d_ff:], preferred_element_type=F32)
    a = (_silu(g) * u).astype(BF16)
    y = jnp.dot(a, wd_ref[...], preferred_element_type=F32)
    o_ref[...] = _layer_norm(alpha * x + (MACARON_WEIGHT * gate) * y, lng_ref[...], lnb_ref[...])


def _ffn(x2, mods, seq, w_gu, w_down, ln_g, ln_b, mod_base, alpha):
    m, d = x2.shape
    d_ff = w_down.shape[0]
    tm = _row_tile(m, seq)
    rpm = seq // tm if mods.shape[0] > 1 else m // tm
    return pl.pallas_call(
        functools.partial(_ffn_kernel, mod_base=mod_base, d_ff=d_ff, alpha=alpha),
        grid=(m // tm,),
        in_specs=[pl.BlockSpec((tm, d), lambda i: (i, 0)),
                  pl.BlockSpec((1, N_MOD, d), lambda i: (i // rpm, 0, 0)),
                  _resident((d, 2 * d_ff)), _resident((d_ff, d)),
                  _resident((1, d)), _resident((1, d))],
        out_specs=pl.BlockSpec((tm, d), lambda i: (i, 0)),
        out_shape=jax.ShapeDtypeStruct((m, d), F32),
        compiler_params=_params("parallel"),
        name="ffn_sublayer",
    )(x2, mods, w_gu, w_down, ln_g.reshape(1, d), ln_b.reshape(1, d))


def _dw_conv_rows(pad_ref, w_ref, j, row0, rows):
    taps = w_ref.shape[0]
    lanes = slice(j * V7X_LANES, (j + 1) * V7X_LANES)
    base = CONV_PAD + row0 - taps // 2
    acc = w_ref[0:1, lanes] * pad_ref[j, base:base + rows, :]
    for k in range(1, taps):
        acc = acc + w_ref[k:k + 1, lanes] * pad_ref[j, base + k:base + k + rows, :]
    return acc


def _even_front_kernel(xp_ref, x_ref, xn_ref, mods_ref, w_ref, alog_ref, dtb_ref, wq_ref, wc_ref, bc_ref,
                       qkv_ref, z_ref, gb_ref, y_ref, qpad_ref, upad_ref,
                       *, n_qkv, n_z, n_cf, tiles_per_seq, conv_rows):
    tm = x_ref.shape[0]
    i = pl.program_id(0)
    first = i % tiles_per_seq == 0
    last = i % tiles_per_seq == tiles_per_seq - 1
    shift = mods_ref[0, 3:4, :]
    scale = mods_ref[0, 4:5, :]
    x = jnp.concatenate([xp_ref[...], x_ref[...], xn_ref[...]], axis=0)
    h = (x * (1.0 + scale) + shift).astype(BF16)
    body = slice(CONV_PAD, CONV_PAD + tm)
    tail = slice(CONV_PAD + tm, CONV_PAD + tm + CONV_PAD)
    proj = lambda c0, n: jnp.dot(h, w_ref[:, c0:c0 + n], preferred_element_type=F32)

    def fill_pad(pad_ref, j, val):
        pad_ref[j, 0:CONV_PAD, :] = jnp.where(first, 0.0, val[0:CONV_PAD])
        pad_ref[j, body, :] = val[body]
        pad_ref[j, tail, :] = jnp.where(last, 0.0, val[tail])

    o_cf = n_qkv + n_z + V7X_LANES
    u = proj(o_cf, n_cf) * _sigmoid(proj(o_cf + n_cf, n_cf))
    for j in range(n_cf // V7X_LANES):
        lanes = slice(j * V7X_LANES, (j + 1) * V7X_LANES)
        fill_pad(upad_ref, j, u[:, lanes])
        for r0 in range(0, tm, conv_rows):
            y_ref[r0:r0 + conv_rows, lanes] = (_dw_conv_rows(upad_ref, wc_ref, j, r0, conv_rows)
                                               + bc_ref[:, lanes])

    group = 2 * V7X_LANES
    for c0 in range(0, n_qkv, group):
        yq = proj(c0, group)
        for jj in range(group // V7X_LANES):
            j = c0 // V7X_LANES + jj
            lanes = slice(j * V7X_LANES, (j + 1) * V7X_LANES)
            fill_pad(qpad_ref, j, yq[:, jj * V7X_LANES:(jj + 1) * V7X_LANES])
            for r0 in range(0, tm, conv_rows):
                t = _silu(_dw_conv_rows(qpad_ref, wq_ref, j, r0, conv_rows))
                if j < 2 * GDN_HEADS:
                    t = t * lax.rsqrt(jnp.sum(t * t, axis=-1, keepdims=True) + RMS_EPS)
                    if j < GDN_HEADS:
                        t = t * (GDN_HEAD_DIM ** -0.5)
                qkv_ref[r0:r0 + conv_rows, lanes] = t

    za = jnp.dot(h[body], w_ref[:, n_qkv:o_cf], preferred_element_type=F32)
    z_ref[...] = za[:, :n_z]
    ab = za[:, n_z:]
    lane = lax.broadcasted_iota(jnp.int32, ab.shape, 1)
    g = -jnp.exp(alog_ref[...]) * _softplus(ab + dtb_ref[...])
    gb_ref[...] = jnp.where(lane < 2 * GDN_HEADS, g, _sigmoid(ab))


def _even_front(x2, mods, seq, w_pad, alog_row, dtb_row, w_qkv_conv, w_cf_conv, b_cf_conv, n_qkv, n_z, n_cf):
    m, d = x2.shape
    tm = _row_tile(m, seq)
    rpm = seq // tm if mods.shape[0] > 1 else m // tm
    hpt = tm // CONV_PAD
    n_halo = m // CONV_PAD
    assert max(w_qkv_conv.shape[0], w_cf_conv.shape[0]) // 2 <= CONV_PAD and tm % CONV_PAD == 0
    row = lambda w: pl.BlockSpec((tm, w), lambda i: (i, 0))
    return pl.pallas_call(
        functools.partial(_even_front_kernel, n_qkv=n_qkv, n_z=n_z, n_cf=n_cf, tiles_per_seq=seq // tm,
                          conv_rows=min(CONV_ROWS, tm)),
        grid=(m // tm,),
        in_specs=[pl.BlockSpec((CONV_PAD, d), lambda i: (jnp.maximum(i * hpt - 1, 0), 0)),
                  row(d),
                  pl.BlockSpec((CONV_PAD, d), lambda i: (jnp.minimum((i + 1) * hpt, n_halo - 1), 0)),
                  pl.BlockSpec((1, N_MOD, d), lambda i: (i // rpm, 0, 0)),
                  _resident(w_pad.shape), _resident((1, V7X_LANES)), _resident((1, V7X_LANES)),
                  _resident(w_qkv_conv.shape), _resident(w_cf_conv.shape), _resident((1, n_cf))],
        out_specs=[row(n_qkv), row(n_z), row(V7X_LANES), row(n_cf)],
        out_shape=[jax.ShapeDtypeStruct((m, n_qkv), F32), jax.ShapeDtypeStruct((m, n_z), F32),
                   jax.ShapeDtypeStruct((m, V7X_LANES), F32), jax.ShapeDtypeStruct((m, n_cf), F32)],
        scratch_shapes=[pltpu.VMEM((n_qkv // V7X_LANES, tm + 2 * CONV_PAD, V7X_LANES), F32),
                        pltpu.VMEM((n_cf // V7X_LANES, tm + 2 * CONV_PAD, V7X_LANES), F32)],
        compiler_params=_params("parallel"),
        name="even_front",
    )(x2, x2, x2, mods, w_pad, alog_row, dtb_row, w_qkv_conv, w_cf_conv, b_cf_conv.reshape(1, n_cf))


def _gdn_kernel(qkvf_ref, qkvb_ref, gbf_ref, gbb_ref, s0_ref, of_ref, ob_ref, sout_ref, s_ref,
                *, n_chunks, n_batch):
    nh, dh, ck = GDN_HEADS, GDN_HEAD_DIM, GDN_CHUNK
    width = nh * dh
    t = pl.program_id(1)

    @pl.when(t == 0)
    def _():
        s_ref[...] = s0_ref[...]

    row = lax.broadcasted_iota(jnp.int32, (ck, ck), 0)
    col = lax.broadcasted_iota(jnp.int32, (ck, ck), 1)
    eye = (row == col).astype(F32)
    masks = ((row >= col, row > col), (row <= col, row < col))
    n_double = ck.bit_length() - 2

    dirs = ((qkvf_ref, gbf_ref, of_ref), (qkvb_ref, gbb_ref, ob_ref))
    chains = [(bb, d, h) for bb in range(n_batch) for d in range(2) for h in range(nh)]

    def chunk_step(ci, carry):
        r0s = [pl.multiple_of((ci if d == 0 else n_chunks - 1 - ci) * ck, ck) for d in range(2)]
        gbcs, gcs, gcts = {}, {}, {}
        for bb in range(n_batch):
            for d, (_, gb_ref, _) in enumerate(dirs):
                gbc = gb_ref[bb, pl.ds(r0s[d], ck), :]
                gc_all = jnp.dot(masks[d][0].astype(F32), gbc, precision=lax.Precision.HIGHEST,
                                 preferred_element_type=F32)
                gbcs[bb, d], gcs[bb, d], gcts[bb, d] = gbc, gc_all, gc_all.T

        st = []
        for bb, d, h in chains:
            lane = d * nh + h
            qkv_ref = dirs[d][0]
            mask = masks[d][0]
            rows = pl.ds(r0s[d], ck)
            gcol = gcs[bb, d][:, lane:lane + 1]
            grow = gcts[bb, d][lane:lane + 1, :]
            beta = gbcs[bb, d][:, 2 * nh + lane:2 * nh + lane + 1]
            q = qkv_ref[bb, rows, h * dh:(h + 1) * dh]
            k = qkv_ref[bb, rows, width + h * dh:width + (h + 1) * dh]
            v = qkv_ref[bb, rows, 2 * width + h * dh:2 * width + (h + 1) * dh]
            decay = jnp.where(mask, jnp.exp(jnp.where(mask, gcol - grow, 0.0)), 0.0)
            kb = k * beta
            eg = jnp.exp(gcol)
            last = ck - 1 if d == 0 else 0
            glast = gcol[last:last + 1, :]
            st.append(dict(bb=bb, d=d, h=h, lane=lane, rows=rows, q=q, k=k, decay=decay, kb=kb, eg=eg,
                           glast=glast, gcol=gcol,
                           rhs=jnp.concatenate([v * beta, kb * eg], axis=1)))
        for c in st:
            c["kkqk"] = _bdot_nt(jnp.concatenate([c["kb"], c["q"]], axis=0), c["k"])
        for c in st:
            c["mp"] = jnp.where(masks[c["d"]][1], c["kkqk"][:ck] * c["decay"], 0.0)
            c["attn"] = c["kkqk"][ck:] * c["decay"]
            c["inv"] = eye - c["mp"]
        for _ in range(n_double):
            for c in st:
                c["mp"] = _bdot(c["mp"], c["mp"])
            for c in st:
                c["inv"] = c["inv"] + _bdot(c["inv"], c["mp"])
        for c in st:
            c["uw"] = _bdot(c["inv"], c["rhs"])
        for c in st:
            c["s"] = s_ref[c["bb"], c["lane"]]
            c["wq_s"] = _bdot(jnp.concatenate([c["uw"][:, dh:], c["q"] * c["eg"]], axis=0), c["s"])
        for c in st:
            c["v_new"] = c["uw"][:, :dh] - c["wq_s"][:ck]
            c["o"] = c["wq_s"][ck:] + _bdot(c["attn"], c["v_new"])
        for c in st:
            kt = c["k"] * jnp.exp(c["glast"] - c["gcol"])
            s_ref[c["bb"], c["lane"]] = c["s"] * jnp.exp(c["glast"]) + _bdot(kt.T, c["v_new"])
        for c in st:
            dirs[c["d"]][2][c["bb"], c["rows"], c["h"] * dh:(c["h"] + 1) * dh] = c["o"]
        return carry

    lax.fori_loop(0, n_chunks, chunk_step, 0)

    @pl.when(t == pl.num_programs(1) - 1)
    def _():
        sout_ref[...] = s_ref[...]


def _gdn_scan(qkv3, gb3, s0):
    b, seq, w3 = qkv3.shape
    nh, dh = GDN_HEADS, GDN_HEAD_DIM
    lt = min(GDN_TILE, seq)
    nb = GDN_BATCH_GROUP if b % GDN_BATCH_GROUP == 0 else 1
    assert seq % lt == 0 and lt % GDN_CHUNK == 0
    nt = seq // lt
    fwd = lambda i, t: (i, t, 0)
    bwd = lambda i, t: (i, nt - 1 - t, 0)
    st = pl.BlockSpec((nb, 2 * nh, dh, dh), lambda i, t: (i, 0, 0, 0))
    return pl.pallas_call(
        functools.partial(_gdn_kernel, n_chunks=lt // GDN_CHUNK, n_batch=nb),
        grid=(b // nb, nt),
        in_specs=[pl.BlockSpec((nb, lt, w3), fwd), pl.BlockSpec((nb, lt, w3), bwd),
                  pl.BlockSpec((nb, lt, V7X_LANES), fwd), pl.BlockSpec((nb, lt, V7X_LANES), bwd), st],
        out_specs=[pl.BlockSpec((nb, lt, nh * dh), fwd), pl.BlockSpec((nb, lt, nh * dh), bwd), st],
        out_shape=[jax.ShapeDtypeStruct((b, seq, nh * dh), F32),
                   jax.ShapeDtypeStruct((b, seq, nh * dh), F32),
                   jax.ShapeDtypeStruct((b, 2 * nh, dh, dh), F32)],
        scratch_shapes=[pltpu.VMEM((nb, 2 * nh, dh, dh), F32)],
        compiler_params=_params("parallel", "arbitrary"),
        name="gdn_scan",
    )(qkv3, qkv3, gb3, gb3, s0)


def _even_merge_kernel(of_ref, ob_ref, z_ref, y_ref, x_ref, mods_ref, onorm_ref, clng_ref, clnb_ref,
                       w1_ref, w2_ref, lng_ref, lnb_ref, o_ref, *, alpha):
    dh = GDN_HEAD_DIM
    o = of_ref[...] + ob_ref[...]
    on = jnp.concatenate([_rms_norm(o[:, h * dh:(h + 1) * dh], onorm_ref[...])
                          for h in range(GDN_HEADS)], axis=1)
    a = on * _silu(z_ref[...])
    cf = _silu(_layer_norm(y_ref[...], clng_ref[...], clnb_ref[...]))
    mix = _bdot(a, w1_ref[...]) + _bdot(cf, w2_ref[...])
    gate = mods_ref[0, 5:6, :]
    o_ref[...] = _layer_norm(alpha * x_ref[...] + gate * mix, lng_ref[...], lnb_ref[...])


def _even_merge(of2, ob2, z2, y2, x2, mods, seq, onorm, clng, clnb, w1, w2, ln_g, ln_b, alpha):
    m, d = x2.shape
    n = of2.shape[1]
    tm = _row_tile(m, seq)
    rpm = seq // tm if mods.shape[0] > 1 else m // tm
    row = lambda w: pl.BlockSpec((tm, w), lambda i: (i, 0))
    return pl.pallas_call(
        functools.partial(_even_merge_kernel, alpha=alpha),
        grid=(m // tm,),
        in_specs=[row(n), row(n), row(n), row(n), row(d),
                  pl.BlockSpec((1, N_MOD, d), lambda i: (i // rpm, 0, 0)),
                  _resident((1, GDN_HEAD_DIM)), _resident((1, n)), _resident((1, n)),
                  _resident(w1.shape), _resident(w2.shape), _resident((1, d)), _resident((1, d))],
        out_specs=row(d),
        out_shape=jax.ShapeDtypeStruct((m, d), F32),
        compiler_params=_params("parallel"),
        name="even_merge_out_proj",
    )(of2, ob2, z2, y2, x2, mods, onorm.reshape(1, -1), clng.reshape(1, n), clnb.reshape(1, n),
      w1, w2, ln_g.reshape(1, d), ln_b.reshape(1, d))


def _attn_proj_kernel(x_ref, mods_ref, w_ref, qn_ref, kn_ref, cos_ref, sin_ref, q_ref, k_ref, v_ref,
                      *, rope):
    hd = ATT_HEAD_DIM
    x = x_ref[...]
    shift = mods_ref[0, 3:4, :]
    scale = mods_ref[0, 4:5, :]
    h = (x * (1.0 + scale) + shift).astype(BF16)

    def head(t, gain, post):
        t = _rms_norm(t, gain)
        if rope:
            t = t * cos_ref[...] + pltpu.roll(t, hd // 2, axis=1) * sin_ref[...]
        return (t * post).astype(BF16)

    pair = 2 * hd
    n_q, n_kv = ATT_HEADS * hd, ATT_KV_HEADS * hd
    for c0 in range(0, n_q + n_kv, pair):
        y = jnp.dot(h, w_ref[:, c0:c0 + pair], preferred_element_type=F32)
        for jj in range(2):
            t = y[:, jj * hd:(jj + 1) * hd]
            c = c0 + jj * hd
            if c < n_q:
                q_ref[:, c:c + hd] = head(t, qn_ref[...], hd ** -0.5 * LOG2_E)
            else:
                k_ref[:, c - n_q:c - n_q + hd] = head(t, kn_ref[...], 1.0)
    v_ref[...] = jnp.dot(h, w_ref[:, n_q + n_kv:], preferred_element_type=F32).astype(BF16)


def _attn_proj(x2, mods, seq, w, qn, kn, cos, sin, rope):
    m, d = x2.shape
    hd = ATT_HEAD_DIM
    tm = _row_tile(m, seq)
    rpm = seq // tm if mods.shape[0] > 1 else m // tm
    ntab = cos.shape[0] // tm
    row = lambda w_: pl.BlockSpec((tm, w_), lambda i: (i, 0))
    tab = pl.BlockSpec((tm, hd), lambda i: (i % ntab, 0))
    return pl.pallas_call(
        functools.partial(_attn_proj_kernel, rope=rope),
        grid=(m // tm,),
        in_specs=[row(d), pl.BlockSpec((1, N_MOD, d), lambda i: (i // rpm, 0, 0)),
                  _resident(w.shape), _resident((1, hd)), _resident((1, hd)), tab, tab],
        out_specs=[row(ATT_HEADS * hd), row(ATT_KV_HEADS * hd), row(ATT_KV_HEADS * hd)],
        out_shape=[jax.ShapeDtypeStruct((m, ATT_HEADS * hd), BF16),
                   jax.ShapeDtypeStruct((m, ATT_KV_HEADS * hd), BF16),
                   jax.ShapeDtypeStruct((m, ATT_KV_HEADS * hd), BF16)],
        compiler_params=_params("parallel"),
        name="attn_in_proj",
    )(x2, mods, w, qn.reshape(1, hd), kn.reshape(1, hd), cos, sin)


def _attn_kernel(q_ref, kl_ref, vl_ref, kc_ref, vc_ref, o_ref, s_ref, vx_ref, *, tk, subtract_max):
    hd = ATT_HEAD_DIM
    seq, ct = kl_ref.shape[1], kc_ref.shape[1]
    tq = q_ref.shape[1]

    @pl.when(pl.program_id(2) == 0)
    def _():
        vx_ref[0:seq, 0:hd] = vl_ref[0]
        vx_ref[seq:seq + ct, 0:hd] = vc_ref[0]
        vx_ref[:, hd:2 * hd] = jnp.ones((seq + ct, hd), BF16)

    chunks = [(kl_ref, j * tk, min(tk, seq - j * tk), j * tk) for j in range(pl.cdiv(seq, tk))]
    chunks += [(kc_ref, j * tk, min(tk, ct - j * tk), seq + j * tk) for j in range(pl.cdiv(ct, tk))]

    if not subtract_max:
        q4 = jnp.concatenate([q_ref[0, :, g * hd:(g + 1) * hd] for g in range(ATT_GROUP)], axis=0)
        acc = jnp.zeros((ATT_GROUP * tq, 2 * hd), F32)
        for k_ref, start, size, off in chunks:
            p = jnp.exp2(_bdot_nt(q4, k_ref[0, start:start + size, :])).astype(BF16)
            acc = acc + jnp.dot(p, vx_ref[off:off + size, :], preferred_element_type=F32)
        out = (acc[:, :hd] / acc[:, hd:hd + 1]).astype(BF16)
        for g in range(ATT_GROUP):
            o_ref[0, :, g * hd:(g + 1) * hd] = out[g * tq:(g + 1) * tq]
        return

    for g in range(ATT_GROUP):
        q = q_ref[0, :, g * hd:(g + 1) * hd]
        buf = g % 2
        mrun = jnp.full((tq, V7X_LANES), -jnp.inf, F32)
        for k_ref, start, size, off in chunks:
            s = _bdot_nt(q, k_ref[0, start:start + size, :])
            s_ref[buf, :, off:off + size] = s
            for c in range(size // V7X_LANES):
                mrun = jnp.maximum(mrun, s[:, c * V7X_LANES:(c + 1) * V7X_LANES])
        mx = jnp.max(mrun, axis=-1, keepdims=True)
        acc = jnp.zeros((tq, 2 * hd), F32)
        for _, _, size, off in chunks:
            p = jnp.exp2(s_ref[buf, :, off:off + size] - mx).astype(BF16)
            acc = acc + jnp.dot(p, vx_ref[off:off + size, :], preferred_element_type=F32)
        o_ref[0, :, g * hd:(g + 1) * hd] = (acc[:, :hd] / acc[:, hd:hd + 1]).astype(BF16)


def _attention_call(q3, kl3, vl3, kc3, vc3, subtract_max):
    b, seq, _ = q3.shape
    ct = kc3.shape[1]
    hd = ATT_HEAD_DIM
    tq = min(ATT_Q_TILE_ROWMAX if subtract_max else ATT_Q_TILE, seq)
    tk = min(ATT_K_TILE_ROWMAX if subtract_max else ATT_K_TILE, seq)
    assert seq % V7X_LANES == 0 and ct % V7X_LANES == 0
    gw = ATT_GROUP * hd
    kv = lambda n: pl.BlockSpec((1, n, hd), lambda i, g, t: (i, 0, g))
    s_shape = (2, tq, seq + ct) if subtract_max else (2, V7X_SUBLANES, V7X_LANES)
    return pl.pallas_call(
        functools.partial(_attn_kernel, tk=tk, subtract_max=subtract_max),
        grid=(b, ATT_KV_HEADS, seq // tq),
        in_specs=[pl.BlockSpec((1, tq, gw), lambda i, g, t: (i, t, g)),
                  kv(seq), kv(seq), kv(ct), kv(ct)],
        out_specs=pl.BlockSpec((1, tq, gw), lambda i, g, t: (i, t, g)),
        out_shape=jax.ShapeDtypeStruct((b, seq, ATT_HEADS * hd), BF16),
        scratch_shapes=[pltpu.VMEM(s_shape, F32), pltpu.VMEM((seq + ct, 2 * hd), BF16)],
        compiler_params=_params("parallel", "parallel", "arbitrary"),
        name="gqa_attention" if subtract_max else "gqa_attention_bounded",
    )(q3, kl3, vl3, kc3, vc3)


def _attention(q3, kl3, vl3, kc3, vc3, score_bound):
    args = (q3, kl3, vl3, kc3, vc3)
    return lax.cond(score_bound < ATT_SAFE_SCORE_BOUND,
                    lambda *a: _attention_call(*a, subtract_max=False),
                    lambda *a: _attention_call(*a, subtract_max=True), *args)


def _out_proj_kernel(a_ref, x_ref, mods_ref, w_ref, lng_ref, lnb_ref, o_ref, *, alpha):
    mix = jnp.dot(a_ref[...], w_ref[...], preferred_element_type=F32)
    gate = mods_ref[0, 5:6, :]
    o_ref[...] = _layer_norm(alpha * x_ref[...] + gate * mix, lng_ref[...], lnb_ref[...])


def _out_proj(a2, x2, mods, seq, w, ln_g, ln_b, alpha):
    m, d = x2.shape
    n = a2.shape[1]
    tm = _row_tile(m, seq)
    rpm = seq // tm if mods.shape[0] > 1 else m // tm
    row = lambda w_: pl.BlockSpec((tm, w_), lambda i: (i, 0))
    return pl.pallas_call(
        functools.partial(_out_proj_kernel, alpha=alpha),
        grid=(m // tm,),
        in_specs=[row(n), row(d), pl.BlockSpec((1, N_MOD, d), lambda i: (i // rpm, 0, 0)),
                  _resident(w.shape), _resident((1, d)), _resident((1, d))],
        out_specs=row(d),
        out_shape=jax.ShapeDtypeStruct((m, d), F32),
        compiler_params=_params("parallel"),
        name="attn_out_proj",
    )(a2, x2, mods, w, ln_g.reshape(1, d), ln_b.reshape(1, d))


def _rope_tables(seq):
    rows = seq // GRID_W
    row = jnp.repeat(jnp.arange(rows, dtype=F32), GRID_W)
    col = jnp.tile(jnp.arange(GRID_W, dtype=F32), rows)
    n_freq = ATT_HEAD_DIM // 4
    inv_freq = jnp.float32(ROPE_THETA) ** (-jnp.arange(n_freq, dtype=F32) / n_freq)
    ang = jnp.concatenate([row[:, None] * inv_freq, col[:, None] * inv_freq], axis=-1)
    cos, sin = jnp.cos(ang), jnp.sin(ang)
    return jnp.concatenate([cos, cos], axis=-1), jnp.concatenate([-sin, sin], axis=-1)


def _split_pairs(n_heads):
    hd = ATT_HEAD_DIM
    within = jnp.concatenate([jnp.arange(0, hd, 2), jnp.arange(1, hd, 2)])
    return (jnp.arange(n_heads)[:, None] * hd + within[None, :]).reshape(-1)


def _lane_row(v):
    flat = v.astype(F32).reshape(-1)
    return jnp.zeros((1, V7X_LANES), F32).at[0, :flat.shape[0]].set(flat)


def kernel(x, c, ctx, c_ctx, ada_w, ada_b, ln_g, ln_b, ffn_w_gu, ffn_w_down, even_w_in, even_qkv_conv,
           gdn_a_log, gdn_dt_bias, gdn_out_norm, cf_dw_conv, cf_dw_bias, cf_ln_g, cf_ln_b, even_w_out,
           attn_w_in, attn_q_norm, attn_k_norm, attn_w_out):
    b, seq, d = x.shape
    ct = ctx.shape[1]
    depth = ada_w.shape[0]
    alpha = (2.0 * depth) ** 0.25
    nh, dh = GDN_HEADS, GDN_HEAD_DIM
    gw = nh * dh
    assert b + 1 <= COND_ROWS and depth <= 2

    cond = jnp.zeros((COND_ROWS, d), F32).at[:b].set(c).at[b].set(c_ctx)
    mods = _ada(cond, ada_w, ada_b)
    xl = x.reshape(b * seq, d)
    xc = ctx.reshape(b * ct, d)

    for layer in range(depth):
        last = layer == depth - 1
        i = layer // 2
        ml = mods[layer, :b].reshape(b, N_MOD, d)
        mc = mods[layer, b:b + 1].reshape(1, N_MOD, d)
        wgu = ffn_w_gu[layer].astype(BF16)
        wdn = ffn_w_down[layer].astype(BF16)
        lg, lb = ln_g[layer], ln_b[layer]

        xl = _ffn(xl, ml, seq, wgu[0], wdn[0], lg[0], lb[0], 0, alpha)
        xc = _ffn(xc, mc, ct, wgu[0], wdn[0], lg[0], lb[0], 0, alpha)

        if layer % 2 == 0:
            w_in = even_w_in[i]
            n_ab = 4 * nh
            n_cf = (w_in.shape[1] - 4 * gw - n_ab) // 2
            w_pad = jnp.concatenate(
                [w_in[:, :4 * gw],
                 jnp.pad(w_in[:, 4 * gw:4 * gw + n_ab], ((0, 0), (0, V7X_LANES - n_ab))),
                 w_in[:, 4 * gw + n_ab:]], axis=1).astype(BF16)
            alog_row, dtb_row = _lane_row(gdn_a_log[i]), _lane_row(gdn_dt_bias[i])
            w_out = even_w_out[i].astype(BF16)

            def mixer_in(x2, m_, n_seq):
                qkv, z, gb, y = _even_front(x2, m_, n_seq, w_pad, alog_row, dtb_row, even_qkv_conv[i],
                                            cf_dw_conv[i], cf_dw_bias[i], 3 * gw, gw, n_cf)
                return qkv.reshape(b, n_seq, 3 * gw), z, gb.reshape(b, n_seq, V7X_LANES), y

            def mixer_out(of, ob, z, y, x2, m_, n_seq):
                return _even_merge(of.reshape(-1, gw), ob.reshape(-1, gw), z, y, x2, m_, n_seq,
                                   gdn_out_norm[i], cf_ln_g[i], cf_ln_b[i], w_out[:gw], w_out[gw:],
                                   lg[1], lb[1], alpha)

            qkv_c, z_c, gb_c, y_c = mixer_in(xc, mc, ct)
            qkv_l, z_l, gb_l, y_l = mixer_in(xl, ml, seq)
            of_c, ob_c, s_ctx = _gdn_scan(qkv_c, gb_c, jnp.zeros((b, 2 * nh, dh, dh), F32))
            of_l, ob_l, _ = _gdn_scan(qkv_l, gb_l, s_ctx)
            xl = mixer_out(of_l, ob_l, z_l, y_l, xl, ml, seq)
            if not last:
                xc = mixer_out(of_c, ob_c, z_c, y_c, xc, mc, ct)
        else:
            assert last, "context attention output is only needed before a further layer"
            hd = ATT_HEAD_DIM
            nq, nkv = ATT_HEADS * hd, ATT_KV_HEADS * hd
            w_in = attn_w_in[i]
            w_perm = jnp.concatenate([w_in[:, :nq][:, _split_pairs(ATT_HEADS)],
                                      w_in[:, nq:nq + nkv][:, _split_pairs(ATT_KV_HEADS)],
                                      w_in[:, nq + nkv:]], axis=1).astype(BF16)
            within = _split_pairs(1)
            qn, kn = attn_q_norm[i][within], attn_k_norm[i][within]
            cos, sin = _rope_tables(seq)
            q_l, k_l, v_l = _attn_proj(xl, ml, seq, w_perm, qn, kn, cos, sin, True)
            _, k_c, v_c = _attn_proj(xc, mc, ct, w_perm, qn, kn, cos, sin, False)
            score_bound = (hd ** 0.5 * (1.0 + 2.0 ** -6) * jnp.max(jnp.abs(attn_q_norm[i]))
                           * jnp.max(jnp.abs(attn_k_norm[i])))
            o = _attention(q_l.reshape(b, seq, nq), k_l.reshape(b, seq, nkv), v_l.reshape(b, seq, nkv),
                           k_c.reshape(b, ct, nkv), v_c.reshape(b, ct, nkv), score_bound)
            xl = _out_proj(o.reshape(b * seq, nq), xl, ml, seq, attn_w_out[i].astype(BF16),
                           lg[1], lb[1], alpha)

        xl = _ffn(xl, ml, seq, wgu[1], wdn[1], lg[2], lb[2], 6, alpha)
        if not last:
            xc = _ffn(xc, mc, ct, wgu[1], wdn[1], lg[2], lb[2], 6, alpha)

    return xl.reshape(b, seq, d)
```

```python
import functools

import jax
import jax.numpy as jnp
from jax import lax
from jax.experimental import pallas as pl
from jax.experimental.pallas import tpu as pltpu

F32 = jnp.float32
BF16 = jnp.bfloat16

GRID_W = 64
LN_EPS = 1e-5
RMS_EPS = 1e-6
N_MOD = 9
MACARON_WEIGHT = 0.5
GDN_HEADS = 4
GDN_HEAD_DIM = 128
GDN_CHUNK = 64
ATT_HEADS = 8
ATT_KV_HEADS = 2
ATT_GROUP = ATT_HEADS // ATT_KV_HEADS
ATT_HEAD_DIM = 128
ROPE_THETA = 10000.0

V7X_LANES = 128
V7X_SUBLANES = 8
V7X_VMEM_LIMIT_BYTES = 56 * 1024 * 1024

COND_ROWS = 16
CONV_PAD = 16
ROW_TILE = 512
CONV_ROWS = 256
GDN_TILE = 256
GDN_BATCH_GROUP = 4
ATT_Q_TILE = 1024
ATT_K_TILE = 256
ATT_Q_TILE_ROWMAX = 256
ATT_K_TILE_ROWMAX = 512
LOG2_E = 1.4426950408889634
ATT_SAFE_SCORE_BOUND = 40.0


def _sigmoid(x):
    return 1.0 / (1.0 + jnp.exp(-x))


def _silu(x):
    return x * _sigmoid(x)


def _softplus(x):
    return jnp.maximum(x, 0.0) + jnp.log1p(jnp.exp(-jnp.abs(x)))


def _layer_norm(x, g, b):
    mu = jnp.mean(x, axis=-1, keepdims=True)
    xc = x - mu
    var = jnp.mean(xc * xc, axis=-1, keepdims=True)
    return xc * lax.rsqrt(var + LN_EPS) * g + b


def _rms_norm(x, g):
    return x * lax.rsqrt(jnp.mean(x * x, axis=-1, keepdims=True) + RMS_EPS) * g


def _bdot(a, b):
    return jnp.dot(a.astype(BF16), b.astype(BF16), preferred_element_type=F32)


def _bdot_nt(a, b):
    return lax.dot_general(a.astype(BF16), b.astype(BF16), (((1,), (1,)), ((), ())),
                           preferred_element_type=F32)


def _params(*semantics):
    return pltpu.CompilerParams(dimension_semantics=semantics,
                                vmem_limit_bytes=V7X_VMEM_LIMIT_BYTES)


def _resident(shape):
    nd = len(shape)
    return pl.BlockSpec(shape, lambda *_: (0,) * nd, pipeline_mode=pl.Buffered(1))


def _row_tile(m, seq):
    t = min(ROW_TILE, seq)
    assert m % t == 0 and seq % t == 0
    return t


def _ada_kernel(c_ref, w_ref, b_ref, o_ref):
    c = c_ref[...]
    o_ref[0] = _bdot(_silu(c), w_ref[0]) + b_ref[0]


def _ada(cond, ada_w, ada_b):
    depth, d, nd = ada_w.shape
    return pl.pallas_call(
        _ada_kernel,
        grid=(depth, nd // d),
        in_specs=[pl.BlockSpec((COND_ROWS, d), lambda l, j: (0, 0)),
                  pl.BlockSpec((1, d, d), lambda l, j: (l, 0, j)),
                  pl.BlockSpec((1, 1, d), lambda l, j: (l, 0, j))],
        out_specs=pl.BlockSpec((1, COND_ROWS, d), lambda l, j: (l, 0, j)),
        out_shape=jax.ShapeDtypeStruct((depth, COND_ROWS, nd), F32),
        compiler_params=_params("parallel", "parallel"),
        name="ada_modulation",
    )(cond, ada_w, ada_b.reshape(depth, 1, nd))


def _ffn_kernel(x_ref, mods_ref, wgu_ref, wd_ref, lng_ref, lnb_ref, o_ref, *, mod_base, d_ff, alpha):
    x = x_ref[...]
    shift = mods_ref[0, mod_base:mod_base + 1, :]
    scale = mods_ref[0, mod_base + 1:mod_base + 2, :]
    gate = mods_ref[0, mod_base + 2:mod_base + 3, :]
    h = (x * (1.0 + scale) + shift).astype(BF16)
    g = jnp.dot(h, wgu_ref[:, :d_ff], preferred_element_type=F32)
    u = jnp.dot(h, wgu_ref[:, d_ff:], preferred_element_type=F32)
    a = (_silu(g) * u).astype(BF16)
    y = jnp.dot(a, wd_ref[...], preferred_element_type=F32)
    o_ref[...] = _layer_norm(alpha * x + (MACARON_WEIGHT * gate) * y, lng_ref[...], lnb_ref[...])


def _ffn(x2, mods, seq, w_gu, w_down, which, ln_g, ln_b, mod_base, alpha):
    m, d = x2.shape
    d_ff = w_down.shape[2]
    tm = _row_tile(m, seq)
    rpm = seq // tm if mods.shape[0] > 1 else m // tm
    pick = lambda shape: pl.BlockSpec((None, None) + shape, lambda i: which + (0, 0),
                                      pipeline_mode=pl.Buffered(1))
    return pl.pallas_call(
        functools.partial(_ffn_kernel, mod_base=mod_base, d_ff=d_ff, alpha=alpha),
        grid=(m // tm,),
        in_specs=[pl.BlockSpec((tm, d), lambda i: (i, 0)),
                  pl.BlockSpec((1, N_MOD, d), lambda i: (i // rpm, 0, 0)),
                  pick((d, 2 * d_ff)), pick((d_ff, d)),
                  _resident((1, d)), _resident((1, d))],
        out_specs=pl.BlockSpec((tm, d), lambda i: (i, 0)),
        out_shape=jax.ShapeDtypeStruct((m, d), F32),
        compiler_params=_params("parallel"),
        name="ffn_sublayer",
    )(x2, mods, w_gu, w_down, ln_g.reshape(1, d), ln_b.reshape(1, d))


def _dw_conv_rows(pad_ref, w_ref, j, row0, rows):
    taps = w_ref.shape[0]
    lanes = slice(j * V7X_LANES, (j + 1) * V7X_LANES)
    base = CONV_PAD + row0 - taps // 2
    acc = w_ref[0:1, lanes] * pad_ref[j, base:base + rows, :]
    for k in range(1, taps):
        acc = acc + w_ref[k:k + 1, lanes] * pad_ref[j, base + k:base + k + rows, :]
    return acc


def _even_front_kernel(xp_ref, x_ref, xn_ref, mods_ref, w_ref, alog_ref, dtb_ref, wq_ref, wc_ref, bc_ref,
                       qkv_ref, z_ref, gb_ref, y_ref, qpad_ref, upad_ref,
                       *, n_qkv, n_z, n_cf, tiles_per_seq, conv_rows):
    tm = x_ref.shape[0]
    i = pl.program_id(0)
    first = i % tiles_per_seq == 0
    last = i % tiles_per_seq == tiles_per_seq - 1
    shift = mods_ref[0, 3:4, :]
    scale = mods_ref[0, 4:5, :]
    x = jnp.concatenate([xp_ref[...], x_ref[...], xn_ref[...]], axis=0)
    h = (x * (1.0 + scale) + shift).astype(BF16)
    body = slice(CONV_PAD, CONV_PAD + tm)
    tail = slice(CONV_PAD + tm, CONV_PAD + tm + CONV_PAD)
    proj = lambda c0, n: jnp.dot(h, w_ref[:, c0:c0 + n], preferred_element_type=F32)

    def fill_pad(pad_ref, j, val):
        pad_ref[j, 0:CONV_PAD, :] = jnp.where(first, 0.0, val[0:CONV_PAD])
        pad_ref[j, body, :] = val[body]
        pad_ref[j, tail, :] = jnp.where(last, 0.0, val[tail])

    o_cf = n_qkv + n_z + V7X_LANES
    u = proj(o_cf, n_cf) * _sigmoid(proj(o_cf + n_cf, n_cf))
    for j in range(n_cf // V7X_LANES):
        lanes = slice(j * V7X_LANES, (j + 1) * V7X_LANES)
        fill_pad(upad_ref, j, u[:, lanes])
        for r0 in range(0, tm, conv_rows):
            y_ref[r0:r0 + conv_rows, lanes] = (_dw_conv_rows(upad_ref, wc_ref, j, r0, conv_rows)
                                               + bc_ref[:, lanes])

    group = 2 * V7X_LANES
    for c0 in range(0, n_qkv, group):
        yq = proj(c0, group)
        for jj in range(group // V7X_LANES):
            j = c0 // V7X_LANES + jj
            lanes = slice(j * V7X_LANES, (j + 1) * V7X_LANES)
            fill_pad(qpad_ref, j, yq[:, jj * V7X_LANES:(jj + 1) * V7X_LANES])
            for r0 in range(0, tm, conv_rows):
                t = _silu(_dw_conv_rows(qpad_ref, wq_ref, j, r0, conv_rows))
                if j < 2 * GDN_HEADS:
                    t = t * lax.rsqrt(jnp.sum(t * t, axis=-1, keepdims=True) + RMS_EPS)
                    if j < GDN_HEADS:
                        t = t * (GDN_HEAD_DIM ** -0.5)
                qkv_ref[r0:r0 + conv_rows, lanes] = t

    za = jnp.dot(h[body], w_ref[:, n_qkv:o_cf], preferred_element_type=F32)
    z_ref[...] = za[:, :n_z]
    ab = za[:, n_z:]
    lane = lax.broadcasted_iota(jnp.int32, ab.shape, 1)
    g = -jnp.exp(alog_ref[...]) * _softplus(ab + dtb_ref[...])
    gb_ref[...] = jnp.where(lane < 2 * GDN_HEADS, g, _sigmoid(ab))


def _even_front(x2, mods, seq, w_pad, alog_row, dtb_row, w_qkv_conv, w_cf_conv, b_cf_conv, n_qkv, n_z, n_cf):
    m, d = x2.shape
    tm = _row_tile(m, seq)
    rpm = seq // tm if mods.shape[0] > 1 else m // tm
    hpt = tm // CONV_PAD
    n_halo = m // CONV_PAD
    assert max(w_qkv_conv.shape[0], w_cf_conv.shape[0]) // 2 <= CONV_PAD and tm % CONV_PAD == 0
    row = lambda w: pl.BlockSpec((tm, w), lambda i: (i, 0))
    return pl.pallas_call(
        functools.partial(_even_front_kernel, n_qkv=n_qkv, n_z=n_z, n_cf=n_cf, tiles_per_seq=seq // tm,
                          conv_rows=min(CONV_ROWS, tm)),
        grid=(m // tm,),
        in_specs=[pl.BlockSpec((CONV_PAD, d), lambda i: (jnp.maximum(i * hpt - 1, 0), 0)),
                  row(d),
                  pl.BlockSpec((CONV_PAD, d), lambda i: (jnp.minimum((i + 1) * hpt, n_halo - 1), 0)),
                  pl.BlockSpec((1, N_MOD, d), lambda i: (i // rpm, 0, 0)),
                  _resident(w_pad.shape), _resident((1, V7X_LANES)), _resident((1, V7X_LANES)),
                  _resident(w_qkv_conv.shape), _resident(w_cf_conv.shape), _resident((1, n_cf))],
        out_specs=[row(n_qkv), row(n_z), row(V7X_LANES), row(n_cf)],
        out_shape=[jax.ShapeDtypeStruct((m, n_qkv), F32), jax.ShapeDtypeStruct((m, n_z), F32),
                   jax.ShapeDtypeStruct((m, V7X_LANES), F32), jax.ShapeDtypeStruct((m, n_cf), F32)],
        scratch_shapes=[pltpu.VMEM((n_qkv // V7X_LANES, tm + 2 * CONV_PAD, V7X_LANES), F32),
                        pltpu.VMEM((n_cf // V7X_LANES, tm + 2 * CONV_PAD, V7X_LANES), F32)],
        compiler_params=_params("parallel"),
        name="even_front",
    )(x2, x2, x2, mods, w_pad, alog_row, dtb_row, w_qkv_conv, w_cf_conv, b_cf_conv.reshape(1, n_cf))


def _gdn_kernel(qkvf_ref, qkvb_ref, gbf_ref, gbb_ref, s0_ref, of_ref, ob_ref, sout_ref, s_ref,
                *, n_chunks, n_batch):
    nh, dh, ck = GDN_HEADS, GDN_HEAD_DIM, GDN_CHUNK
    width = nh * dh
    t = pl.program_id(1)

    @pl.when(t == 0)
    def _():
        s_ref[...] = s0_ref[...]

    row = lax.broadcasted_iota(jnp.int32, (ck, ck), 0)
    col = lax.broadcasted_iota(jnp.int32, (ck, ck), 1)
    eye = (row == col).astype(F32)
    masks = ((row >= col, row > col), (row <= col, row < col))
    n_double = ck.bit_length() - 2

    dirs = ((qkvf_ref, gbf_ref, of_ref), (qkvb_ref, gbb_ref, ob_ref))
    chains = [(bb, d, h) for bb in range(n_batch) for d in range(2) for h in range(nh)]

    def chunk_step(ci, carry):
        r0s = [pl.multiple_of((ci if d == 0 else n_chunks - 1 - ci) * ck, ck) for d in range(2)]
        gbcs, gcs, gcts = {}, {}, {}
        for bb in range(n_batch):
            for d, (_, gb_ref, _) in enumerate(dirs):
                gbc = gb_ref[bb, pl.ds(r0s[d], ck), :]
                gc_all = jnp.dot(masks[d][0].astype(F32), gbc, precision=lax.Precision.HIGHEST,
                                 preferred_element_type=F32)
                gbcs[bb, d], gcs[bb, d], gcts[bb, d] = gbc, gc_all, gc_all.T

        st = []
        for bb, d, h in chains:
            lane = d * nh + h
            qkv_ref = dirs[d][0]
            mask = masks[d][0]
            rows = pl.ds(r0s[d], ck)
            gcol = gcs[bb, d][:, lane:lane + 1]
            grow = gcts[bb, d][lane:lane + 1, :]
            beta = gbcs[bb, d][:, 2 * nh + lane:2 * nh + lane + 1]
            q = qkv_ref[bb, rows, h * dh:(h + 1) * dh]
            k = qkv_ref[bb, rows, width + h * dh:width + (h + 1) * dh]
            v = qkv_ref[bb, rows, 2 * width + h * dh:2 * width + (h + 1) * dh]
            decay = jnp.where(mask, jnp.exp(jnp.where(mask, gcol - grow, 0.0)), 0.0)
            kb = k * beta
            eg = jnp.exp(gcol)
            last = ck - 1 if d == 0 else 0
            glast = gcol[last:last + 1, :]
            st.append(dict(bb=bb, d=d, h=h, lane=lane, rows=rows, q=q, k=k, decay=decay, kb=kb, eg=eg,
                           glast=glast, gcol=gcol,
                           rhs=jnp.concatenate([v * beta, kb * eg], axis=1)))
        for c in st:
            c["kkqk"] = _bdot_nt(jnp.concatenate([c["kb"], c["q"]], axis=0), c["k"])
        for c in st:
            c["mp"] = jnp.where(masks[c["d"]][1], c["kkqk"][:ck] * c["decay"], 0.0)
            c["attn"] = c["kkqk"][ck:] * c["decay"]
            c["inv"] = eye - c["mp"]
        for c in st:
            c["mp"] = _bdot(c["mp"], c["mp"])
        for _ in range(n_double - 1):
            for c in st:
                both = _bdot(jnp.concatenate([c["inv"], c["mp"]], axis=0), c["mp"])
                c["inv"] = c["inv"] + both[:ck]
                c["mp"] = both[ck:]
        for c in st:
            c["inv"] = c["inv"] + _bdot(c["inv"], c["mp"])
        for c in st:
            c["uw"] = _bdot(c["inv"], c["rhs"])
        for c in st:
            c["s"] = s_ref[c["bb"], c["lane"]]
            c["wq_s"] = _bdot(jnp.concatenate([c["uw"][:, dh:], c["q"] * c["eg"]], axis=0), c["s"])
        for c in st:
            c["v_new"] = c["uw"][:, :dh] - c["wq_s"][:ck]
            kt = c["k"] * jnp.exp(c["glast"] - c["gcol"])
            c["av"] = _bdot(jnp.concatenate([c["attn"], kt.T], axis=0), c["v_new"])
        for c in st:
            s_ref[c["bb"], c["lane"]] = c["s"] * jnp.exp(c["glast"]) + c["av"][ck:]
            dirs[c["d"]][2][c["bb"], c["rows"], c["h"] * dh:(c["h"] + 1) * dh] = (c["wq_s"][ck:]
                                                                                + c["av"][:ck])
        return carry

    lax.fori_loop(0, n_chunks, chunk_step, 0)

    @pl.when(t == pl.num_programs(1) - 1)
    def _():
        sout_ref[...] = s_ref[...]


def _gdn_scan(qkv3, gb3, s0):
    b, seq, w3 = qkv3.shape
    nh, dh = GDN_HEADS, GDN_HEAD_DIM
    lt = min(GDN_TILE, seq)
    nb = GDN_BATCH_GROUP if b % GDN_BATCH_GROUP == 0 else 1
    assert seq % lt == 0 and lt % GDN_CHUNK == 0
    nt = seq // lt
    fwd = lambda i, t: (i, t, 0)
    bwd = lambda i, t: (i, nt - 1 - t, 0)
    st = pl.BlockSpec((nb, 2 * nh, dh, dh), lambda i, t: (i, 0, 0, 0))
    return pl.pallas_call(
        functools.partial(_gdn_kernel, n_chunks=lt // GDN_CHUNK, n_batch=nb),
        grid=(b // nb, nt),
        in_specs=[pl.BlockSpec((nb, lt, w3), fwd), pl.BlockSpec((nb, lt, w3), bwd),
                  pl.BlockSpec((nb, lt, V7X_LANES), fwd), pl.BlockSpec((nb, lt, V7X_LANES), bwd), st],
        out_specs=[pl.BlockSpec((nb, lt, nh * dh), fwd), pl.BlockSpec((nb, lt, nh * dh), bwd), st],
        out_shape=[jax.ShapeDtypeStruct((b, seq, nh * dh), F32),
                   jax.ShapeDtypeStruct((b, seq, nh * dh), F32),
                   jax.ShapeDtypeStruct((b, 2 * nh, dh, dh), F32)],
        scratch_shapes=[pltpu.VMEM((nb, 2 * nh, dh, dh), F32)],
        compiler_params=_params("parallel", "arbitrary"),
        name="gdn_scan",
    )(qkv3, qkv3, gb3, gb3, s0)


def _even_merge_kernel(of_ref, ob_ref, z_ref, y_ref, x_ref, mods_ref, onorm_ref, clng_ref, clnb_ref,
                       w1_ref, w2_ref, lng_ref, lnb_ref, o_ref, *, alpha):
    dh = GDN_HEAD_DIM
    o = of_ref[...] + ob_ref[...]
    on = jnp.concatenate([_rms_norm(o[:, h * dh:(h + 1) * dh], onorm_ref[...])
                          for h in range(GDN_HEADS)], axis=1)
    a = on * _silu(z_ref[...])
    cf = _silu(_layer_norm(y_ref[...], clng_ref[...], clnb_ref[...]))
    mix = _bdot(a, w1_ref[...]) + _bdot(cf, w2_ref[...])
    gate = mods_ref[0, 5:6, :]
    o_ref[...] = _layer_norm(alpha * x_ref[...] + gate * mix, lng_ref[...], lnb_ref[...])


def _even_merge(of2, ob2, z2, y2, x2, mods, seq, onorm, clng, clnb, w1, w2, ln_g, ln_b, alpha):
    m, d = x2.shape
    n = of2.shape[1]
    tm = _row_tile(m, seq)
    rpm = seq // tm if mods.shape[0] > 1 else m // tm
    row = lambda w: pl.BlockSpec((tm, w), lambda i: (i, 0))
    return pl.pallas_call(
        functools.partial(_even_merge_kernel, alpha=alpha),
        grid=(m // tm,),
        in_specs=[row(n), row(n), row(n), row(n), row(d),
                  pl.BlockSpec((1, N_MOD, d), lambda i: (i // rpm, 0, 0)),
                  _resident((1, GDN_HEAD_DIM)), _resident((1, n)), _resident((1, n)),
                  _resident(w1.shape), _resident(w2.shape), _resident((1, d)), _resident((1, d))],
        out_specs=row(d),
        out_shape=jax.ShapeDtypeStruct((m, d), F32),
        compiler_params=_params("parallel"),
        name="even_merge_out_proj",
    )(of2, ob2, z2, y2, x2, mods, onorm.reshape(1, -1), clng.reshape(1, n), clnb.reshape(1, n),
      w1, w2, ln_g.reshape(1, d), ln_b.reshape(1, d))


def _attn_proj_kernel(x_ref, mods_ref, w_ref, qn_ref, kn_ref, cos_ref, sin_ref, q_ref, k_ref, v_ref,
                      y_ref, *, rope):
    hd = ATT_HEAD_DIM
    x = x_ref[...]
    shift = mods_ref[0, 3:4, :]
    scale = mods_ref[0, 4:5, :]
    h = (x * (1.0 + scale) + shift).astype(BF16)

    ones = jnp.ones((2 * hd, hd), BF16)

    def head(t, gain, post):
        sq = t * t
        hi = sq.astype(BF16)
        lo = (sq - hi.astype(F32)).astype(BF16)
        ss = jnp.dot(jnp.concatenate([hi, lo], axis=1), ones, preferred_element_type=F32)
        t = t * lax.rsqrt(ss * (1.0 / hd) + RMS_EPS) * gain
        if rope:
            t = t * cos_ref[...] + pltpu.roll(t, hd // 2, axis=1) * sin_ref[...]
        return (t * post).astype(BF16)

    pair = 2 * hd
    n_q, n_kv = ATT_HEADS * hd, ATT_KV_HEADS * hd
    for c0 in range(0, n_q + n_kv, pair):
        y_ref[:, c0:c0 + pair] = jnp.dot(h, w_ref[:, c0:c0 + pair], preferred_element_type=F32)
    v_ref[...] = jnp.dot(h, w_ref[:, n_q + n_kv:], preferred_element_type=F32).astype(BF16)
    for c in range(0, n_q + n_kv, hd):
        t = y_ref[:, c:c + hd]
        if c < n_q:
            q_ref[:, c:c + hd] = head(t, qn_ref[...], hd ** -0.5 * LOG2_E)
        else:
            k_ref[:, c - n_q:c - n_q + hd] = head(t, kn_ref[...], 1.0)


def _attn_proj(x2, mods, seq, w, qn, kn, cos, sin, rope):
    m, d = x2.shape
    hd = ATT_HEAD_DIM
    tm = _row_tile(m, seq)
    rpm = seq // tm if mods.shape[0] > 1 else m // tm
    ntab = cos.shape[0] // tm
    row = lambda w_: pl.BlockSpec((tm, w_), lambda i: (i, 0))
    tab = pl.BlockSpec((tm, hd), lambda i: (i % ntab, 0))
    return pl.pallas_call(
        functools.partial(_attn_proj_kernel, rope=rope),
        grid=(m // tm,),
        in_specs=[row(d), pl.BlockSpec((1, N_MOD, d), lambda i: (i // rpm, 0, 0)),
                  _resident(w.shape), _resident((1, hd)), _resident((1, hd)), tab, tab],
        out_specs=[row(ATT_HEADS * hd), row(ATT_KV_HEADS * hd), row(ATT_KV_HEADS * hd)],
        out_shape=[jax.ShapeDtypeStruct((m, ATT_HEADS * hd), BF16),
                   jax.ShapeDtypeStruct((m, ATT_KV_HEADS * hd), BF16),
                   jax.ShapeDtypeStruct((m, ATT_KV_HEADS * hd), BF16)],
        scratch_shapes=[pltpu.VMEM((tm, (ATT_HEADS + ATT_KV_HEADS) * hd), F32)],
        compiler_params=_params("parallel"),
        name="attn_in_proj",
    )(x2, mods, w, qn.reshape(1, hd), kn.reshape(1, hd), cos, sin)


def _attn_kernel(q_ref, kl_ref, vl_ref, kc_ref, vc_ref, o_ref, s_ref, vx_ref, *, tk, subtract_max):
    hd = ATT_HEAD_DIM
    seq, ct = kl_ref.shape[1], kc_ref.shape[1]
    tq = q_ref.shape[1]

    @pl.when(pl.program_id(2) == 0)
    def _():
        vx_ref[0:seq, 0:hd] = vl_ref[0]
        vx_ref[seq:seq + ct, 0:hd] = vc_ref[0]
        vx_ref[:, hd:2 * hd] = jnp.ones((seq + ct, hd), BF16)

    chunks = [(kl_ref, j * tk, min(tk, seq - j * tk), j * tk) for j in range(pl.cdiv(seq, tk))]
    chunks += [(kc_ref, j * tk, min(tk, ct - j * tk), seq + j * tk) for j in range(pl.cdiv(ct, tk))]

    if not subtract_max:
        q4 = jnp.concatenate([q_ref[0, :, g * hd:(g + 1) * hd] for g in range(ATT_GROUP)], axis=0)
        acc = jnp.zeros((ATT_GROUP * tq, 2 * hd), F32)
        for k_ref, start, size, off in chunks:
            p = jnp.exp2(_bdot_nt(q4, k_ref[0, start:start + size, :])).astype(BF16)
            acc = acc + jnp.dot(p, vx_ref[off:off + size, :], preferred_element_type=F32)
        out = (acc[:, :hd] / acc[:, hd:hd + 1]).astype(BF16)
        for g in range(ATT_GROUP):
            o_ref[0, :, g * hd:(g + 1) * hd] = out[g * tq:(g + 1) * tq]
        return

    for g in range(ATT_GROUP):
        q = q_ref[0, :, g * hd:(g + 1) * hd]
        buf = g % 2
        mrun = jnp.full((tq, V7X_LANES), -jnp.inf, F32)
        for k_ref, start, size, off in chunks:
            s = _bdot_nt(q, k_ref[0, start:start + size, :])
            s_ref[buf, :, off:off + size] = s
            for c in range(size // V7X_LANES):
                mrun = jnp.maximum(mrun, s[:, c * V7X_LANES:(c + 1) * V7X_LANES])
        mx = jnp.max(mrun, axis=-1, keepdims=True)
        acc = jnp.zeros((tq, 2 * hd), F32)
        for _, _, size, off in chunks:
            p = jnp.exp2(s_ref[buf, :, off:off + size] - mx).astype(BF16)
            acc = acc + jnp.dot(p, vx_ref[off:off + size, :], preferred_element_type=F32)
        o_ref[0, :, g * hd:(g + 1) * hd] = (acc[:, :hd] / acc[:, hd:hd + 1]).astype(BF16)


def _attention_call(q3, kl3, vl3, kc3, vc3, subtract_max):
    b, seq, _ = q3.shape
    ct = kc3.shape[1]
    hd = ATT_HEAD_DIM
    tq = min(ATT_Q_TILE_ROWMAX if subtract_max else ATT_Q_TILE, seq)
    tk = min(ATT_K_TILE_ROWMAX if subtract_max else ATT_K_TILE, seq)
    assert seq % V7X_LANES == 0 and ct % V7X_LANES == 0
    gw = ATT_GROUP * hd
    kv = lambda n: pl.BlockSpec((1, n, hd), lambda i, g, t: (i, 0, g))
    s_shape = (2, tq, seq + ct) if subtract_max else (2, V7X_SUBLANES, V7X_LANES)
    return pl.pallas_call(
        functools.partial(_attn_kernel, tk=tk, subtract_max=subtract_max),
        grid=(b, ATT_KV_HEADS, seq // tq),
        in_specs=[pl.BlockSpec((1, tq, gw), lambda i, g, t: (i, t, g)),
                  kv(seq), kv(seq), kv(ct), kv(ct)],
        out_specs=pl.BlockSpec((1, tq, gw), lambda i, g, t: (i, t, g)),
        out_shape=jax.ShapeDtypeStruct((b, seq, ATT_HEADS * hd), BF16),
        scratch_shapes=[pltpu.VMEM(s_shape, F32), pltpu.VMEM((seq + ct, 2 * hd), BF16)],
        compiler_params=_params("parallel", "parallel", "arbitrary"),
        name="gqa_attention" if subtract_max else "gqa_attention_bounded",
    )(q3, kl3, vl3, kc3, vc3)


def _attention(q3, kl3, vl3, kc3, vc3, score_bound):
    args = (q3, kl3, vl3, kc3, vc3)
    return lax.cond(score_bound < ATT_SAFE_SCORE_BOUND,
                    lambda *a: _attention_call(*a, subtract_max=False),
                    lambda *a: _attention_call(*a, subtract_max=True), *args)


def _out_proj_kernel(a_ref, x_ref, mods_ref, w_ref, lng_ref, lnb_ref, o_ref, *, alpha):
    mix = jnp.dot(a_ref[...], w_ref[...], preferred_element_type=F32)
    gate = mods_ref[0, 5:6, :]
    o_ref[...] = _layer_norm(alpha * x_ref[...] + gate * mix, lng_ref[...], lnb_ref[...])


def _out_proj(a2, x2, mods, seq, w, ln_g, ln_b, alpha):
    m, d = x2.shape
    n = a2.shape[1]
    tm = _row_tile(m, seq)
    rpm = seq // tm if mods.shape[0] > 1 else m // tm
    row = lambda w_: pl.BlockSpec((tm, w_), lambda i: (i, 0))
    return pl.pallas_call(
        functools.partial(_out_proj_kernel, alpha=alpha),
        grid=(m // tm,),
        in_specs=[row(n), row(d), pl.BlockSpec((1, N_MOD, d), lambda i: (i // rpm, 0, 0)),
                  _resident(w.shape), _resident((1, d)), _resident((1, d))],
        out_specs=row(d),
        out_shape=jax.ShapeDtypeStruct((m, d), F32),
        compiler_params=_params("parallel"),
        name="attn_out_proj",
    )(a2, x2, mods, w, ln_g.reshape(1, d), ln_b.reshape(1, d))


def _rope_tables(seq):
    rows = seq // GRID_W
    row = jnp.repeat(jnp.arange(rows, dtype=F32), GRID_W)
    col = jnp.tile(jnp.arange(GRID_W, dtype=F32), rows)
    n_freq = ATT_HEAD_DIM // 4
    inv_freq = jnp.float32(ROPE_THETA) ** (-jnp.arange(n_freq, dtype=F32) / n_freq)
    ang = jnp.concatenate([row[:, None] * inv_freq, col[:, None] * inv_freq], axis=-1)
    cos, sin = jnp.cos(ang), jnp.sin(ang)
    return jnp.concatenate([cos, cos], axis=-1), jnp.concatenate([-sin, sin], axis=-1)


def _split_pairs(n_heads):
    hd = ATT_HEAD_DIM
    within = jnp.concatenate([jnp.arange(0, hd, 2), jnp.arange(1, hd, 2)])
    return (jnp.arange(n_heads)[:, None] * hd + within[None, :]).reshape(-1)


def _lane_row(v):
    flat = v.astype(F32).reshape(-1)
    return jnp.zeros((1, V7X_LANES), F32).at[0, :flat.shape[0]].set(flat)


def kernel(x, c, ctx, c_ctx, ada_w, ada_b, ln_g, ln_b, ffn_w_gu, ffn_w_down, even_w_in, even_qkv_conv,
           gdn_a_log, gdn_dt_bias, gdn_out_norm, cf_dw_conv, cf_dw_bias, cf_ln_g, cf_ln_b, even_w_out,
           attn_w_in, attn_q_norm, attn_k_norm, attn_w_out):
    b, seq, d = x.shape
    ct = ctx.shape[1]
    depth = ada_w.shape[0]
    alpha = (2.0 * depth) ** 0.25
    nh, dh = GDN_HEADS, GDN_HEAD_DIM
    gw = nh * dh
    assert b + 1 <= COND_ROWS and depth <= 2

    cond = jnp.zeros((COND_ROWS, d), F32).at[:b].set(c).at[b].set(c_ctx)
    mods = _ada(cond, ada_w, ada_b)
    xl = x.reshape(b * seq, d)
    xc = ctx.reshape(b * ct, d)
    wgu, wdn = ffn_w_gu.astype(BF16), ffn_w_down.astype(BF16)

    for layer in range(depth):
        last = layer == depth - 1
        i = layer // 2
        ml = mods[layer, :b].reshape(b, N_MOD, d)
        mc = mods[layer, b:b + 1].reshape(1, N_MOD, d)
        lg, lb = ln_g[layer], ln_b[layer]

        xl = _ffn(xl, ml, seq, wgu, wdn, (layer, 0), lg[0], lb[0], 0, alpha)
        xc = _ffn(xc, mc, ct, wgu, wdn, (layer, 0), lg[0], lb[0], 0, alpha)

        if layer % 2 == 0:
            w_in = even_w_in[i]
            n_ab = 4 * nh
            n_cf = (w_in.shape[1] - 4 * gw - n_ab) // 2
            w_pad = jnp.concatenate(
                [w_in[:, :4 * gw],
                 jnp.pad(w_in[:, 4 * gw:4 * gw + n_ab], ((0, 0), (0, V7X_LANES - n_ab))),
                 w_in[:, 4 * gw + n_ab:]], axis=1).astype(BF16)
            alog_row, dtb_row = _lane_row(gdn_a_log[i]), _lane_row(gdn_dt_bias[i])
            w_out = even_w_out[i].astype(BF16)

            def mixer_in(x2, m_, n_seq):
                qkv, z, gb, y = _even_front(x2, m_, n_seq, w_pad, alog_row, dtb_row, even_qkv_conv[i],
                                            cf_dw_conv[i], cf_dw_bias[i], 3 * gw, gw, n_cf)
                return qkv.reshape(b, n_seq, 3 * gw), z, gb.reshape(b, n_seq, V7X_LANES), y

            def mixer_out(of, ob, z, y, x2, m_, n_seq):
                return _even_merge(of.reshape(-1, gw), ob.reshape(-1, gw), z, y, x2, m_, n_seq,
                                   gdn_out_norm[i], cf_ln_g[i], cf_ln_b[i], w_out[:gw], w_out[gw:],
                                   lg[1], lb[1], alpha)

            qkv_c, z_c, gb_c, y_c = mixer_in(xc, mc, ct)
            qkv_l, z_l, gb_l, y_l = mixer_in(xl, ml, seq)
            of_c, ob_c, s_ctx = _gdn_scan(qkv_c, gb_c, jnp.zeros((b, 2 * nh, dh, dh), F32))
            of_l, ob_l, _ = _gdn_scan(qkv_l, gb_l, s_ctx)
            xl = mixer_out(of_l, ob_l, z_l, y_l, xl, ml, seq)
            if not last:
                xc = mixer_out(of_c, ob_c, z_c, y_c, xc, mc, ct)
        else:
            assert last, "context attention output is only needed before a further layer"
            hd = ATT_HEAD_DIM
            nq, nkv = ATT_HEADS * hd, ATT_KV_HEADS * hd
            w_in = attn_w_in[i]
            w_perm = jnp.concatenate([w_in[:, :nq][:, _split_pairs(ATT_HEADS)],
                                      w_in[:, nq:nq + nkv][:, _split_pairs(ATT_KV_HEADS)],
                                      w_in[:, nq + nkv:]], axis=1).astype(BF16)
            within = _split_pairs(1)
            qn, kn = attn_q_norm[i][within], attn_k_norm[i][within]
            cos, sin = _rope_tables(seq)
            q_l, k_l, v_l = _attn_proj(xl, ml, seq, w_perm, qn, kn, cos, sin, True)
            _, k_c, v_c = _attn_proj(xc, mc, ct, w_perm, qn, kn, cos, sin, False)
            score_bound = (hd ** 0.5 * (1.0 + 2.0 ** -6) * jnp.max(jnp.abs(attn_q_norm[i]))
                           * jnp.max(jnp.abs(attn_k_norm[i])))
            o = _attention(q_l.reshape(b, seq, nq), k_l.reshape(b, seq, nkv), v_l.reshape(b, seq, nkv),
                           k_c.reshape(b, ct, nkv), v_c.reshape(b, ct, nkv), score_bound)
            xl = _out_proj(o.reshape(b * seq, nq), xl, ml, seq, attn_w_out[i].astype(BF16),
                           lg[1], lb[1], alpha)

        xl = _ffn(xl, ml, seq, wgu, wdn, (layer, 1), lg[2], lb[2], 6, alpha)
        if not last:
            xc = _ffn(xc, mc, ct, wgu, wdn, (layer, 1), lg[2], lb[2], 6, alpha)

    return xl.reshape(b, seq, d)
```

```python
import functools

import jax
import jax.numpy as jnp
from jax import lax
from jax.experimental import pallas as pl
from jax.experimental.pallas import tpu as pltpu

F32 = jnp.float32
BF16 = jnp.bfloat16

GRID_W = 64
LN_EPS = 1e-5
RMS_EPS = 1e-6
N_MOD = 9
MACARON_WEIGHT = 0.5
GDN_HEADS = 4
GDN_HEAD_DIM = 128
GDN_CHUNK = 64
ATT_HEADS = 8
ATT_KV_HEADS = 2
ATT_GROUP = ATT_HEADS // ATT_KV_HEADS
ATT_HEAD_DIM = 128
ROPE_THETA = 10000.0

V7X_LANES = 128
V7X_SUBLANES = 8
V7X_VMEM_LIMIT_BYTES = 56 * 1024 * 1024

COND_ROWS = 16
CONV_PAD = 16
ROW_TILE = 512
FFN_SPLIT = 2
CONV_ROWS = 256
CF_FRONT_TILES = 2
GDN_TILE = 256
GDN_BATCH_GROUP = 4
ATT_Q_TILE = 1024
ATT_K_TILE = 256
ATT_Q_TILE_ROWMAX = 256
ATT_K_TILE_ROWMAX = 512
LOG2_E = 1.4426950408889634
ATT_SAFE_SCORE_BOUND = 40.0


def _sigmoid(x):
    return 1.0 / (1.0 + jnp.exp(-x))


def _silu(x):
    return x * _sigmoid(x)


def _softplus(x):
    return jnp.maximum(x, 0.0) + jnp.log1p(jnp.exp(-jnp.abs(x)))


def _layer_norm(x, g, b):
    mu = jnp.mean(x, axis=-1, keepdims=True)
    xc = x - mu
    var = jnp.mean(xc * xc, axis=-1, keepdims=True)
    return xc * lax.rsqrt(var + LN_EPS) * g + b


def _rms_norm(x, g):
    return x * lax.rsqrt(jnp.mean(x * x, axis=-1, keepdims=True) + RMS_EPS) * g


def _bdot(a, b):
    return jnp.dot(a.astype(BF16), b.astype(BF16), preferred_element_type=F32)


def _bdot_nt(a, b):
    return lax.dot_general(a.astype(BF16), b.astype(BF16), (((1,), (1,)), ((), ())),
                           preferred_element_type=F32)


def _params(*semantics):
    return pltpu.CompilerParams(dimension_semantics=semantics,
                                vmem_limit_bytes=V7X_VMEM_LIMIT_BYTES)


def _resident(shape):
    nd = len(shape)
    return pl.BlockSpec(shape, lambda *_: (0,) * nd, pipeline_mode=pl.Buffered(1))


def _row_tile(m, seq):
    t = min(ROW_TILE, seq)
    assert m % t == 0 and seq % t == 0
    return t


def _ada_kernel(c_ref, w_ref, b_ref, o_ref):
    c = c_ref[...]
    o_ref[0] = _bdot(_silu(c), w_ref[0]) + b_ref[0]


def _ada(cond, ada_w, ada_b):
    depth, d, nd = ada_w.shape
    return pl.pallas_call(
        _ada_kernel,
        grid=(depth, nd // d),
        in_specs=[pl.BlockSpec((COND_ROWS, d), lambda l, j: (0, 0)),
                  pl.BlockSpec((1, d, d), lambda l, j: (l, 0, j)),
                  pl.BlockSpec((1, 1, d), lambda l, j: (l, 0, j))],
        out_specs=pl.BlockSpec((1, COND_ROWS, d), lambda l, j: (l, 0, j)),
        out_shape=jax.ShapeDtypeStruct((depth, COND_ROWS, nd), F32),
        compiler_params=_params("parallel", "parallel"),
        name="ada_modulation",
    )(cond, ada_w, ada_b.reshape(depth, 1, nd))


def _ffn_kernel(x_ref, mods_ref, *refs, mod_base, d_ff, alpha, mixer_proj):
    if mixer_proj:
        a_ref, wo_ref, lng1_ref, lnb1_ref = refs[:4]
        refs = refs[4:]
    wgu_ref, wd_ref, lng_ref, lnb_ref, o_ref = refs
    shift = mods_ref[0, mod_base:mod_base + 1, :]
    scale = mods_ref[0, mod_base + 1:mod_base + 2, :]
    gate = mods_ref[0, mod_base + 2:mod_base + 3, :]
    tm = x_ref.shape[0]
    halves = [slice(r, r + tm // FFN_SPLIT) for r in range(0, tm, tm // FFN_SPLIT)]
    xs = [x_ref[rows, :] for rows in halves]
    if mixer_proj:
        mixes = [jnp.dot(a_ref[rows, :], wo_ref[...], preferred_element_type=F32) for rows in halves]
        xs = [_layer_norm(alpha * x + mods_ref[0, 5:6, :] * mix, lng1_ref[...], lnb1_ref[...])
              for x, mix in zip(xs, mixes)]
    hs = [(x * (1.0 + scale) + shift).astype(BF16) for x in xs]
    gs = [jnp.dot(h, wgu_ref[:, :d_ff], preferred_element_type=F32) for h in hs]
    us = [jnp.dot(h, wgu_ref[:, d_ff:], preferred_element_type=F32) for h in hs]
    acts = [(_silu(g) * u).astype(BF16) for g, u in zip(gs, us)]
    ys = [jnp.dot(a, wd_ref[...], preferred_element_type=F32) for a in acts]
    for rows, x, y in zip(halves, xs, ys):
        o_ref[rows, :] = _layer_norm(alpha * x + (MACARON_WEIGHT * gate) * y, lng_ref[...], lnb_ref[...])


def _ffn(x2, mods, seq, w_gu, w_down, which, ln_g, ln_b, mod_base, alpha, mixer_proj=None):
    m, d = x2.shape
    d_ff = w_down.shape[2]
    tm = _row_tile(m, seq)
    rpm = seq // tm if mods.shape[0] > 1 else m // tm
    pick = lambda shape: pl.BlockSpec((None, None) + shape, lambda i: which + (0, 0),
                                      pipeline_mode=pl.Buffered(1))
    row = lambda w: pl.BlockSpec((tm, w), lambda i: (i, 0))
    pre_specs, pre_args = [], []
    if mixer_proj is not None:
        a2, w_out, ln_g1, ln_b1 = mixer_proj
        pre_specs = [row(a2.shape[1]), _resident(w_out.shape), _resident((1, d)), _resident((1, d))]
        pre_args = [a2, w_out, ln_g1.reshape(1, d), ln_b1.reshape(1, d)]
    return pl.pallas_call(
        functools.partial(_ffn_kernel, mod_base=mod_base, d_ff=d_ff, alpha=alpha,
                          mixer_proj=mixer_proj is not None),
        grid=(m // tm,),
        in_specs=[row(d), pl.BlockSpec((1, N_MOD, d), lambda i: (i // rpm, 0, 0))] + pre_specs
                 + [pick((d, 2 * d_ff)), pick((d_ff, d)), _resident((1, d)), _resident((1, d))],
        out_specs=row(d),
        out_shape=jax.ShapeDtypeStruct((m, d), F32),
        compiler_params=_params("parallel"),
        name="ffn_sublayer" if mixer_proj is None else "attn_out_proj_ffn",
    )(x2, mods, *pre_args, w_gu, w_down, ln_g.reshape(1, d), ln_b.reshape(1, d))


def _dw_conv_rows(pad_ref, w_ref, j, row0, rows, w_tile=None):
    taps = w_ref.shape[0]
    w_tile = j if w_tile is None else w_tile
    lanes = slice(w_tile * V7X_LANES, (w_tile + 1) * V7X_LANES)
    base = CONV_PAD + row0 - taps // 2
    acc = w_ref[0:1, lanes] * pad_ref[j, base:base + rows, :]
    for k in range(1, taps):
        acc = acc + w_ref[k:k + 1, lanes] * pad_ref[j, base + k:base + k + rows, :]
    return acc


def _fill_conv_pad(pad_ref, j, val, first, last):
    tm = val.shape[0] - 2 * CONV_PAD
    pad_ref[j, 0:CONV_PAD, :] = jnp.where(first, 0.0, val[0:CONV_PAD])
    pad_ref[j, CONV_PAD:CONV_PAD + tm, :] = val[CONV_PAD:CONV_PAD + tm]
    pad_ref[j, CONV_PAD + tm:, :] = jnp.where(last, 0.0, val[CONV_PAD + tm:])


def _even_front_kernel(xp_ref, x_ref, xn_ref, mods_ref, w_ref, alog_ref, dtb_ref, wq_ref, wc_ref, bc_ref,
                       qkv_ref, z_ref, gb_ref, yf_ref, u_ref, qpad_ref, upad_ref,
                       *, n_qkv, n_z, n_cf, tiles_per_seq, conv_rows):
    tm = x_ref.shape[0]
    i = pl.program_id(0)
    first = i % tiles_per_seq == 0
    last = i % tiles_per_seq == tiles_per_seq - 1
    shift = mods_ref[0, 3:4, :]
    scale = mods_ref[0, 4:5, :]
    x = jnp.concatenate([xp_ref[...], x_ref[...], xn_ref[...]], axis=0)
    h = (x * (1.0 + scale) + shift).astype(BF16)
    body = slice(CONV_PAD, CONV_PAD + tm)
    proj = lambda c0, n: jnp.dot(h, w_ref[:, c0:c0 + n], preferred_element_type=F32)

    o_cf = n_qkv + n_z + V7X_LANES
    n_front = CF_FRONT_TILES * V7X_LANES
    uf = proj(o_cf, n_front) * _sigmoid(proj(o_cf + n_cf, n_front))
    for j in range(CF_FRONT_TILES):
        lanes = slice(j * V7X_LANES, (j + 1) * V7X_LANES)
        _fill_conv_pad(upad_ref, j, uf[:, lanes], first, last)
        for r0 in range(0, tm, conv_rows):
            yf_ref[r0:r0 + conv_rows, lanes] = (_dw_conv_rows(upad_ref, wc_ref, j, r0, conv_rows)
                                                + bc_ref[:, lanes])

    group = 2 * V7X_LANES
    for c0 in range(0, n_qkv, group):
        yq = proj(c0, group)
        for jj in range(group // V7X_LANES):
            j = c0 // V7X_LANES + jj
            lanes = slice(j * V7X_LANES, (j + 1) * V7X_LANES)
            _fill_conv_pad(qpad_ref, j, yq[:, jj * V7X_LANES:(jj + 1) * V7X_LANES], first, last)
            for r0 in range(0, tm, conv_rows):
                t = _silu(_dw_conv_rows(qpad_ref, wq_ref, j, r0, conv_rows))
                if j < 2 * GDN_HEADS:
                    t = t * lax.rsqrt(jnp.sum(t * t, axis=-1, keepdims=True) + RMS_EPS)
                    if j < GDN_HEADS:
                        t = t * (GDN_HEAD_DIM ** -0.5)
                qkv_ref[r0:r0 + conv_rows, lanes] = t

    hb = h[body]
    za = jnp.dot(hb, w_ref[:, n_qkv:o_cf], preferred_element_type=F32)
    z_ref[...] = za[:, :n_z]
    ab = za[:, n_z:]
    u_ref[...] = (jnp.dot(hb, w_ref[:, o_cf + n_front:o_cf + n_cf], preferred_element_type=F32)
                  * _sigmoid(jnp.dot(hb, w_ref[:, o_cf + n_cf + n_front:], preferred_element_type=F32)))
    lane = lax.broadcasted_iota(jnp.int32, ab.shape, 1)
    g = -jnp.exp(alog_ref[...]) * _softplus(ab + dtb_ref[...])
    gb_ref[...] = jnp.where(lane < 2 * GDN_HEADS, g, _sigmoid(ab))


def _even_front(x2, mods, seq, w_pad, alog_row, dtb_row, w_qkv_conv, w_cf_conv, b_cf_conv, n_qkv, n_z, n_cf):
    m, d = x2.shape
    tm = _row_tile(m, seq)
    rpm = seq // tm if mods.shape[0] > 1 else m // tm
    hpt = tm // CONV_PAD
    n_halo = m // CONV_PAD
    assert max(w_qkv_conv.shape[0], w_cf_conv.shape[0]) // 2 <= CONV_PAD and tm % CONV_PAD == 0
    n_front = CF_FRONT_TILES * V7X_LANES
    row = lambda w: pl.BlockSpec((tm, w), lambda i: (i, 0))
    return pl.pallas_call(
        functools.partial(_even_front_kernel, n_qkv=n_qkv, n_z=n_z, n_cf=n_cf, tiles_per_seq=seq // tm,
                          conv_rows=min(CONV_ROWS, tm)),
        grid=(m // tm,),
        in_specs=[pl.BlockSpec((CONV_PAD, d), lambda i: (jnp.maximum(i * hpt - 1, 0), 0)),
                  row(d),
                  pl.BlockSpec((CONV_PAD, d), lambda i: (jnp.minimum((i + 1) * hpt, n_halo - 1), 0)),
                  pl.BlockSpec((1, N_MOD, d), lambda i: (i // rpm, 0, 0)),
                  _resident(w_pad.shape), _resident((1, V7X_LANES)), _resident((1, V7X_LANES)),
                  _resident(w_qkv_conv.shape), _resident(w_cf_conv.shape), _resident((1, n_cf))],
        out_specs=[row(n_qkv), row(n_z), row(V7X_LANES), row(n_front), row(n_cf - n_front)],
        out_shape=[jax.ShapeDtypeStruct((m, n_qkv), F32), jax.ShapeDtypeStruct((m, n_z), F32),
                   jax.ShapeDtypeStruct((m, V7X_LANES), F32), jax.ShapeDtypeStruct((m, n_front), F32),
                   jax.ShapeDtypeStruct((m, n_cf - n_front), F32)],
        scratch_shapes=[pltpu.VMEM((n_qkv // V7X_LANES, tm + 2 * CONV_PAD, V7X_LANES), F32),
                        pltpu.VMEM((CF_FRONT_TILES, tm + 2 * CONV_PAD, V7X_LANES), F32)],
        compiler_params=_params("parallel"),
        name="even_front",
    )(x2, x2, x2, mods, w_pad, alog_row, dtb_row, w_qkv_conv, w_cf_conv, b_cf_conv.reshape(1, n_cf))


def _gdn_kernel(qkvf_ref, qkvb_ref, gbf_ref, gbb_ref, s0_ref, of_ref, ob_ref, sout_ref, s_ref,
                *, n_chunks, n_batch):
    nh, dh, ck = GDN_HEADS, GDN_HEAD_DIM, GDN_CHUNK
    width = nh * dh
    t = pl.program_id(1)

    @pl.when(t == 0)
    def _():
        s_ref[...] = s0_ref[...]

    row = lax.broadcasted_iota(jnp.int32, (ck, ck), 0)
    col = lax.broadcasted_iota(jnp.int32, (ck, ck), 1)
    eye = (row == col).astype(F32)
    masks = ((row >= col, row > col), (row <= col, row < col))
    n_double = ck.bit_length() - 2

    dirs = ((qkvf_ref, gbf_ref, of_ref), (qkvb_ref, gbb_ref, ob_ref))
    chains = [(bb, d, h) for bb in range(n_batch) for d in range(2) for h in range(nh)]

    def chunk_step(ci, carry):
        r0s = [pl.multiple_of((ci if d == 0 else n_chunks - 1 - ci) * ck, ck) for d in range(2)]
        gbcs, gcs, gcts = {}, {}, {}
        for bb in range(n_batch):
            for d, (_, gb_ref, _) in enumerate(dirs):
                gbc = gb_ref[bb, pl.ds(r0s[d], ck), :]
                gc_all = jnp.dot(masks[d][0].astype(F32), gbc, precision=lax.Precision.HIGHEST,
                                 preferred_element_type=F32)
                gbcs[bb, d], gcs[bb, d], gcts[bb, d] = gbc, gc_all, gc_all.T

        st = []
        for bb, d, h in chains:
            lane = d * nh + h
            qkv_ref = dirs[d][0]
            mask = masks[d][0]
            rows = pl.ds(r0s[d], ck)
            gcol = gcs[bb, d][:, lane:lane + 1]
            grow = gcts[bb, d][lane:lane + 1, :]
            beta = gbcs[bb, d][:, 2 * nh + lane:2 * nh + lane + 1]
            q = qkv_ref[bb, rows, h * dh:(h + 1) * dh]
            k = qkv_ref[bb, rows, width + h * dh:width + (h + 1) * dh]
            v = qkv_ref[bb, rows, 2 * width + h * dh:2 * width + (h + 1) * dh]
            decay = jnp.where(mask, jnp.exp(jnp.where(mask, gcol - grow, 0.0)), 0.0)
            kb = k * beta
            eg = jnp.exp(gcol)
            last = ck - 1 if d == 0 else 0
            glast = gcol[last:last + 1, :]
            st.append(dict(bb=bb, d=d, h=h, lane=lane, rows=rows, q=q, k=k, decay=decay, kb=kb, eg=eg,
                           glast=glast, gcol=gcol,
                           rhs=jnp.concatenate([v * beta, kb * eg], axis=1)))
        for c in st:
            c["kkqk"] = _bdot_nt(jnp.concatenate([c["kb"], c["q"]], axis=0), c["k"])
        for c in st:
            c["mp"] = jnp.where(masks[c["d"]][1], c["kkqk"][:ck] * c["decay"], 0.0)
            c["attn"] = c["kkqk"][ck:] * c["decay"]
            c["inv"] = eye - c["mp"]
        for c in st:
            c["mp"] = _bdot(c["mp"], c["mp"])
        for _ in range(n_double - 1):
            for c in st:
                both = _bdot(jnp.concatenate([c["inv"], c["mp"]], axis=0), c["mp"])
                c["inv"] = c["inv"] + both[:ck]
                c["mp"] = both[ck:]
        for c in st:
            c["inv"] = c["inv"] + _bdot(c["inv"], c["mp"])
        for c in st:
            c["uw"] = _bdot(c["inv"], c["rhs"])
        for c in st:
            c["s"] = s_ref[c["bb"], c["lane"]]
            c["wq_s"] = _bdot(jnp.concatenate([c["uw"][:, dh:], c["q"] * c["eg"]], axis=0), c["s"])
        for c in st:
            c["v_new"] = c["uw"][:, :dh] - c["wq_s"][:ck]
            kt = c["k"] * jnp.exp(c["glast"] - c["gcol"])
            c["av"] = _bdot(jnp.concatenate([c["attn"], kt.T], axis=0), c["v_new"])
        for c in st:
            s_ref[c["bb"], c["lane"]] = c["s"] * jnp.exp(c["glast"]) + c["av"][ck:]
            dirs[c["d"]][2][c["bb"], c["rows"], c["h"] * dh:(c["h"] + 1) * dh] = (c["wq_s"][ck:]
                                                                                + c["av"][:ck])
        return carry

    lax.fori_loop(0, n_chunks, chunk_step, 0)

    @pl.when(t == pl.num_programs(1) - 1)
    def _():
        sout_ref[...] = s_ref[...]


def _gdn_scan(qkv3, gb3, s0):
    b, seq, w3 = qkv3.shape
    nh, dh = GDN_HEADS, GDN_HEAD_DIM
    lt = min(GDN_TILE, seq)
    nb = GDN_BATCH_GROUP if b % GDN_BATCH_GROUP == 0 else 1
    assert seq % lt == 0 and lt % GDN_CHUNK == 0
    nt = seq // lt
    fwd = lambda i, t: (i, t, 0)
    bwd = lambda i, t: (i, nt - 1 - t, 0)
    st = pl.BlockSpec((nb, 2 * nh, dh, dh), lambda i, t: (i, 0, 0, 0))
    return pl.pallas_call(
        functools.partial(_gdn_kernel, n_chunks=lt // GDN_CHUNK, n_batch=nb),
        grid=(b // nb, nt),
        in_specs=[pl.BlockSpec((nb, lt, w3), fwd), pl.BlockSpec((nb, lt, w3), bwd),
                  pl.BlockSpec((nb, lt, V7X_LANES), fwd), pl.BlockSpec((nb, lt, V7X_LANES), bwd), st],
        out_specs=[pl.BlockSpec((nb, lt, nh * dh), fwd), pl.BlockSpec((nb, lt, nh * dh), bwd), st],
        out_shape=[jax.ShapeDtypeStruct((b, seq, nh * dh), F32),
                   jax.ShapeDtypeStruct((b, seq, nh * dh), F32),
                   jax.ShapeDtypeStruct((b, 2 * nh, dh, dh), F32)],
        scratch_shapes=[pltpu.VMEM((nb, 2 * nh, dh, dh), F32)],
        compiler_params=_params("parallel", "arbitrary"),
        name="gdn_scan",
    )(qkv3, qkv3, gb3, gb3, s0)


def _even_merge_kernel(of_ref, ob_ref, z_ref, yf_ref, up_ref, u_ref, un_ref, x_ref, mods_ref, onorm_ref, wc_ref,
                       bc_ref, clng_ref, clnb_ref, w1_ref, w2_ref, lng_ref, lnb_ref, o_ref, upad_ref, y_ref,
                       *, alpha, tiles_per_seq, conv_rows):
    dh = GDN_HEAD_DIM
    tm = u_ref.shape[0]
    i = pl.program_id(0)
    first = i % tiles_per_seq == 0
    last = i % tiles_per_seq == tiles_per_seq - 1
    n_front = yf_ref.shape[1]
    y_ref[:, 0:n_front] = yf_ref[...]
    u = jnp.concatenate([up_ref[...], u_ref[...], un_ref[...]], axis=0)
    for j in range(upad_ref.shape[0]):
        _fill_conv_pad(upad_ref, j, u[:, j * V7X_LANES:(j + 1) * V7X_LANES], first, last)
        jw = CF_FRONT_TILES + j
        lanes = slice(jw * V7X_LANES, (jw + 1) * V7X_LANES)
        for r0 in range(0, tm, conv_rows):
            y_ref[r0:r0 + conv_rows, lanes] = (_dw_conv_rows(upad_ref, wc_ref, j, r0, conv_rows, jw)
                                               + bc_ref[:, lanes])
    o = of_ref[...] + ob_ref[...]
    on = jnp.concatenate([_rms_norm(o[:, h * dh:(h + 1) * dh], onorm_ref[...])
                          for h in range(GDN_HEADS)], axis=1)
    a = on * _silu(z_ref[...])
    cf = _silu(_layer_norm(y_ref[...], clng_ref[...], clnb_ref[...]))
    mix = _bdot(a, w1_ref[...]) + _bdot(cf, w2_ref[...])
    gate = mods_ref[0, 5:6, :]
    o_ref[...] = _layer_norm(alpha * x_ref[...] + gate * mix, lng_ref[...], lnb_ref[...])


def _even_merge(of2, ob2, z2, yf2, u2, x2, mods, seq, onorm, w_cf_conv, b_cf_conv, clng, clnb, w1, w2, ln_g,
                ln_b, alpha):
    m, d = x2.shape
    n = of2.shape[1]
    nf, nr = yf2.shape[1], u2.shape[1]
    tm = _row_tile(m, seq)
    rpm = seq // tm if mods.shape[0] > 1 else m // tm
    hpt = tm // CONV_PAD
    n_halo = m // CONV_PAD
    assert w_cf_conv.shape[0] // 2 <= CONV_PAD and tm % CONV_PAD == 0
    row = lambda w: pl.BlockSpec((tm, w), lambda i: (i, 0))
    return pl.pallas_call(
        functools.partial(_even_merge_kernel, alpha=alpha, tiles_per_seq=seq // tm,
                          conv_rows=min(CONV_ROWS, tm)),
        grid=(m // tm,),
        in_specs=[row(n), row(n), row(n), row(nf),
                  pl.BlockSpec((CONV_PAD, nr), lambda i: (jnp.maximum(i * hpt - 1, 0), 0)),
                  row(nr),
                  pl.BlockSpec((CONV_PAD, nr), lambda i: (jnp.minimum((i + 1) * hpt, n_halo - 1), 0)),
                  row(d),
                  pl.BlockSpec((1, N_MOD, d), lambda i: (i // rpm, 0, 0)),
                  _resident((1, GDN_HEAD_DIM)), _resident(w_cf_conv.shape), _resident((1, n)),
                  _resident((1, n)), _resident((1, n)),
                  _resident(w1.shape), _resident(w2.shape), _resident((1, d)), _resident((1, d))],
        out_specs=row(d),
        out_shape=jax.ShapeDtypeStruct((m, d), F32),
        scratch_shapes=[pltpu.VMEM((nr // V7X_LANES, tm + 2 * CONV_PAD, V7X_LANES), F32),
                        pltpu.VMEM((tm, n), F32)],
        compiler_params=_params("parallel"),
        name="even_merge_out_proj",
    )(of2, ob2, z2, yf2, u2, u2, u2, x2, mods, onorm.reshape(1, -1), w_cf_conv, b_cf_conv.reshape(1, n),
      clng.reshape(1, n), clnb.reshape(1, n), w1, w2, ln_g.reshape(1, d), ln_b.reshape(1, d))


def _attn_proj_kernel(x_ref, mods_ref, w_ref, qn_ref, kn_ref, cos_ref, sin_ref, q_ref, k_ref, v_ref,
                      y_ref, *, rope):
    hd = ATT_HEAD_DIM
    x = x_ref[...]
    shift = mods_ref[0, 3:4, :]
    scale = mods_ref[0, 4:5, :]
    h = (x * (1.0 + scale) + shift).astype(BF16)

    ones = jnp.ones((2 * hd, hd), BF16)

    def head(t, gain, post):
        sq = t * t
        hi = sq.astype(BF16)
        lo = (sq - hi.astype(F32)).astype(BF16)
        ss = jnp.dot(jnp.concatenate([hi, lo], axis=1), ones, preferred_element_type=F32)
        t = t * lax.rsqrt(ss * (1.0 / hd) + RMS_EPS) * gain
        if rope:
            t = t * cos_ref[...] + pltpu.roll(t, hd // 2, axis=1) * sin_ref[...]
        return (t * post).astype(BF16)

    pair = 2 * hd
    n_q, n_kv = ATT_HEADS * hd, ATT_KV_HEADS * hd
    for c0 in range(0, n_q + n_kv, pair):
        y_ref[:, c0:c0 + pair] = jnp.dot(h, w_ref[:, c0:c0 + pair], preferred_element_type=F32)
    v_ref[...] = jnp.dot(h, w_ref[:, n_q + n_kv:], preferred_element_type=F32).astype(BF16)
    for c in range(0, n_q + n_kv, hd):
        t = y_ref[:, c:c + hd]
        if c < n_q:
            q_ref[:, c:c + hd] = head(t, qn_ref[...], hd ** -0.5 * LOG2_E)
        else:
            k_ref[:, c - n_q:c - n_q + hd] = head(t, kn_ref[...], 1.0)


def _attn_proj(x2, mods, seq, w, qn, kn, cos, sin, rope):
    m, d = x2.shape
    hd = ATT_HEAD_DIM
    tm = _row_tile(m, seq)
    rpm = seq // tm if mods.shape[0] > 1 else m // tm
    ntab = cos.shape[0] // tm
    row = lambda w_: pl.BlockSpec((tm, w_), lambda i: (i, 0))
    tab = pl.BlockSpec((tm, hd), lambda i: (i % ntab, 0))
    return pl.pallas_call(
        functools.partial(_attn_proj_kernel, rope=rope),
        grid=(m // tm,),
        in_specs=[row(d), pl.BlockSpec((1, N_MOD, d), lambda i: (i // rpm, 0, 0)),
                  _resident(w.shape), _resident((1, hd)), _resident((1, hd)), tab, tab],
        out_specs=[row(ATT_HEADS * hd), row(ATT_KV_HEADS * hd), row(ATT_KV_HEADS * hd)],
        out_shape=[jax.ShapeDtypeStruct((m, ATT_HEADS * hd), BF16),
                   jax.ShapeDtypeStruct((m, ATT_KV_HEADS * hd), BF16),
                   jax.ShapeDtypeStruct((m, ATT_KV_HEADS * hd), BF16)],
        scratch_shapes=[pltpu.VMEM((tm, (ATT_HEADS + ATT_KV_HEADS) * hd), F32)],
        compiler_params=_params("parallel"),
        name="attn_in_proj",
    )(x2, mods, w, qn.reshape(1, hd), kn.reshape(1, hd), cos, sin)


def _attn_kernel(q_ref, kl_ref, vl_ref, kc_ref, vc_ref, o_ref, s_ref, vx_ref, *, tk, subtract_max):
    hd = ATT_HEAD_DIM
    seq, ct = kl_ref.shape[1], kc_ref.shape[1]
    tq = q_ref.shape[1]

    @pl.when(pl.program_id(2) == 0)
    def _():
        vx_ref[0:seq, 0:hd] = vl_ref[0]
        vx_ref[seq:seq + ct, 0:hd] = vc_ref[0]
        vx_ref[:, hd:2 * hd] = jnp.ones((seq + ct, hd), BF16)

    chunks = [(kl_ref, j * tk, min(tk, seq - j * tk), j * tk) for j in range(pl.cdiv(seq, tk))]
    chunks += [(kc_ref, j * tk, min(tk, ct - j * tk), seq + j * tk) for j in range(pl.cdiv(ct, tk))]

    if not subtract_max:
        q4 = jnp.concatenate([q_ref[0, :, g * hd:(g + 1) * hd] for g in range(ATT_GROUP)], axis=0)
        acc = jnp.zeros((ATT_GROUP * tq, 2 * hd), F32)
        for k_ref, start, size, off in chunks:
            p = jnp.exp2(_bdot_nt(q4, k_ref[0, start:start + size, :])).astype(BF16)
            acc = acc + jnp.dot(p, vx_ref[off:off + size, :], preferred_element_type=F32)
        out = (acc[:, :hd] / acc[:, hd:hd + 1]).astype(BF16)
        for g in range(ATT_GROUP):
            o_ref[0, :, g * hd:(g + 1) * hd] = out[g * tq:(g + 1) * tq]
        return

    for g in range(ATT_GROUP):
        q = q_ref[0, :, g * hd:(g + 1) * hd]
        buf = g % 2
        mrun = jnp.full((tq, V7X_LANES), -jnp.inf, F32)
        for k_ref, start, size, off in chunks:
            s = _bdot_nt(q, k_ref[0, start:start + size, :])
            s_ref[buf, :, off:off + size] = s
            for c in range(size // V7X_LANES):
                mrun = jnp.maximum(mrun, s[:, c * V7X_LANES:(c + 1) * V7X_LANES])
        mx = jnp.max(mrun, axis=-1, keepdims=True)
        acc = jnp.zeros((tq, 2 * hd), F32)
        for _, _, size, off in chunks:
            p = jnp.exp2(s_ref[buf, :, off:off + size] - mx).astype(BF16)
            acc = acc + jnp.dot(p, vx_ref[off:off + size, :], preferred_element_type=F32)
        o_ref[0, :, g * hd:(g + 1) * hd] = (acc[:, :hd] / acc[:, hd:hd + 1]).astype(BF16)


def _attention_call(q3, kl3, vl3, kc3, vc3, subtract_max):
    b, seq, _ = q3.shape
    ct = kc3.shape[1]
    hd = ATT_HEAD_DIM
    tq = min(ATT_Q_TILE_ROWMAX if subtract_max else ATT_Q_TILE, seq)
    tk = min(ATT_K_TILE_ROWMAX if subtract_max else ATT_K_TILE, seq)
    assert seq % V7X_LANES == 0 and ct % V7X_LANES == 0
    gw = ATT_GROUP * hd
    kv = lambda n: pl.BlockSpec((1, n, hd), lambda i, g, t: (i, 0, g))
    s_shape = (2, tq, seq + ct) if subtract_max else (2, V7X_SUBLANES, V7X_LANES)
    return pl.pallas_call(
        functools.partial(_attn_kernel, tk=tk, subtract_max=subtract_max),
        grid=(b, ATT_KV_HEADS, seq // tq),
        in_specs=[pl.BlockSpec((1, tq, gw), lambda i, g, t: (i, t, g)),
                  kv(seq), kv(seq), kv(ct), kv(ct)],
        out_specs=pl.BlockSpec((1, tq, gw), lambda i, g, t: (i, t, g)),
        out_shape=jax.ShapeDtypeStruct((b, seq, ATT_HEADS * hd), BF16),
        scratch_shapes=[pltpu.VMEM(s_shape, F32), pltpu.VMEM((seq + ct, 2 * hd), BF16)],
        compiler_params=_params("parallel", "parallel", "arbitrary"),
        name="gqa_attention" if subtract_max else "gqa_attention_bounded",
    )(q3, kl3, vl3, kc3, vc3)


def _attention(q3, kl3, vl3, kc3, vc3, score_bound):
    args = (q3, kl3, vl3, kc3, vc3)
    return lax.cond(score_bound < ATT_SAFE_SCORE_BOUND,
                    lambda *a: _attention_call(*a, subtract_max=False),
                    lambda *a: _attention_call(*a, subtract_max=True), *args)


def _rope_tables(seq):
    rows = seq // GRID_W
    row = jnp.repeat(jnp.arange(rows, dtype=F32), GRID_W)
    col = jnp.tile(jnp.arange(GRID_W, dtype=F32), rows)
    n_freq = ATT_HEAD_DIM // 4
    inv_freq = jnp.float32(ROPE_THETA) ** (-jnp.arange(n_freq, dtype=F32) / n_freq)
    ang = jnp.concatenate([row[:, None] * inv_freq, col[:, None] * inv_freq], axis=-1)
    cos, sin = jnp.cos(ang), jnp.sin(ang)
    return jnp.concatenate([cos, cos], axis=-1), jnp.concatenate([-sin, sin], axis=-1)


def _split_pairs(n_heads):
    hd = ATT_HEAD_DIM
    within = jnp.concatenate([jnp.arange(0, hd, 2), jnp.arange(1, hd, 2)])
    return (jnp.arange(n_heads)[:, None] * hd + within[None, :]).reshape(-1)


def _lane_row(v):
    flat = v.astype(F32).reshape(-1)
    return jnp.zeros((1, V7X_LANES), F32).at[0, :flat.shape[0]].set(flat)


def kernel(x, c, ctx, c_ctx, ada_w, ada_b, ln_g, ln_b, ffn_w_gu, ffn_w_down, even_w_in, even_qkv_conv,
           gdn_a_log, gdn_dt_bias, gdn_out_norm, cf_dw_conv, cf_dw_bias, cf_ln_g, cf_ln_b, even_w_out,
           attn_w_in, attn_q_norm, attn_k_norm, attn_w_out):
    b, seq, d = x.shape
    ct = ctx.shape[1]
    depth = ada_w.shape[0]
    alpha = (2.0 * depth) ** 0.25
    nh, dh = GDN_HEADS, GDN_HEAD_DIM
    gw = nh * dh
    assert b + 1 <= COND_ROWS and depth <= 2

    cond = jnp.zeros((COND_ROWS, d), F32).at[:b].set(c).at[b].set(c_ctx)
    mods = _ada(cond, ada_w, ada_b)
    xl = x.reshape(b * seq, d)
    xc = ctx.reshape(b * ct, d)
    wgu, wdn = ffn_w_gu.astype(BF16), ffn_w_down.astype(BF16)

    for layer in range(depth):
        last = layer == depth - 1
        i = layer // 2
        ml = mods[layer, :b].reshape(b, N_MOD, d)
        mc = mods[layer, b:b + 1].reshape(1, N_MOD, d)
        lg, lb = ln_g[layer], ln_b[layer]

        xl = _ffn(xl, ml, seq, wgu, wdn, (layer, 0), lg[0], lb[0], 0, alpha)
        xc = _ffn(xc, mc, ct, wgu, wdn, (layer, 0), lg[0], lb[0], 0, alpha)

        mixer_proj = None
        if layer % 2 == 0:
            w_in = even_w_in[i]
            n_ab = 4 * nh
            n_cf = (w_in.shape[1] - 4 * gw - n_ab) // 2
            w_pad = jnp.concatenate(
                [w_in[:, :4 * gw],
                 jnp.pad(w_in[:, 4 * gw:4 * gw + n_ab], ((0, 0), (0, V7X_LANES - n_ab))),
                 w_in[:, 4 * gw + n_ab:]], axis=1).astype(BF16)
            alog_row, dtb_row = _lane_row(gdn_a_log[i]), _lane_row(gdn_dt_bias[i])
            w_out = even_w_out[i].astype(BF16)

            def mixer_in(x2, m_, n_seq):
                qkv, z, gb, yf, u = _even_front(x2, m_, n_seq, w_pad, alog_row, dtb_row, even_qkv_conv[i],
                                                cf_dw_conv[i], cf_dw_bias[i], 3 * gw, gw, n_cf)
                return qkv.reshape(b, n_seq, 3 * gw), z, gb.reshape(b, n_seq, V7X_LANES), (yf, u)

            def mixer_out(of, ob, z, yu, x2, m_, n_seq):
                return _even_merge(of.reshape(-1, gw), ob.reshape(-1, gw), z, yu[0], yu[1], x2, m_, n_seq,
                                   gdn_out_norm[i], cf_dw_conv[i], cf_dw_bias[i], cf_ln_g[i], cf_ln_b[i],
                                   w_out[:gw], w_out[gw:], lg[1], lb[1], alpha)

            qkv_c, z_c, gb_c, y_c = mixer_in(xc, mc, ct)
            qkv_l, z_l, gb_l, y_l = mixer_in(xl, ml, seq)
            of_c, ob_c, s_ctx = _gdn_scan(qkv_c, gb_c, jnp.zeros((b, 2 * nh, dh, dh), F32))
            of_l, ob_l, _ = _gdn_scan(qkv_l, gb_l, s_ctx)
            xl = mixer_out(of_l, ob_l, z_l, y_l, xl, ml, seq)
            if not last:
                xc = mixer_out(of_c, ob_c, z_c, y_c, xc, mc, ct)
        else:
            assert last, "context attention output is only needed before a further layer"
            hd = ATT_HEAD_DIM
            nq, nkv = ATT_HEADS * hd, ATT_KV_HEADS * hd
            w_in = attn_w_in[i]
            w_perm = jnp.concatenate([w_in[:, :nq][:, _split_pairs(ATT_HEADS)],
                                      w_in[:, nq:nq + nkv][:, _split_pairs(ATT_KV_HEADS)],
                                      w_in[:, nq + nkv:]], axis=1).astype(BF16)
            within = _split_pairs(1)
            qn, kn = attn_q_norm[i][within], attn_k_norm[i][within]
            cos, sin = _rope_tables(seq)
            q_l, k_l, v_l = _attn_proj(xl, ml, seq, w_perm, qn, kn, cos, sin, True)
            _, k_c, v_c = _attn_proj(xc, mc, ct, w_perm, qn, kn, cos, sin, False)
            score_bound = (hd ** 0.5 * (1.0 + 2.0 ** -6) * jnp.max(jnp.abs(attn_q_norm[i]))
                           * jnp.max(jnp.abs(attn_k_norm[i])))
            o = _attention(q_l.reshape(b, seq, nq), k_l.reshape(b, seq, nkv), v_l.reshape(b, seq, nkv),
                           k_c.reshape(b, ct, nkv), v_c.reshape(b, ct, nkv), score_bound)
            mixer_proj = (o.reshape(b * seq, nq), attn_w_out[i].astype(BF16), lg[1], lb[1])

        xl = _ffn(xl, ml, seq, wgu, wdn, (layer, 1), lg[2], lb[2], 6, alpha, mixer_proj)
        if not last:
            xc = _ffn(xc, mc, ct, wgu, wdn, (layer, 1), lg[2], lb[2], 6, alpha)

    return xl.reshape(b, seq, d)
```

```python
import functools

import jax
import jax.numpy as jnp
from jax import lax
from jax.experimental import pallas as pl
from jax.experimental.pallas import tpu as pltpu

F32 = jnp.float32
BF16 = jnp.bfloat16

GRID_W = 64
LN_EPS = 1e-5
RMS_EPS = 1e-6
N_MOD = 9
MACARON_WEIGHT = 0.5
GDN_HEADS = 4
GDN_HEAD_DIM = 128
GDN_CHUNK = 64
ATT_HEADS = 8
ATT_KV_HEADS = 2
ATT_GROUP = ATT_HEADS // ATT_KV_HEADS
ATT_HEAD_DIM = 128
ROPE_THETA = 10000.0

V7X_LANES = 128
V7X_SUBLANES = 8
V7X_VMEM_LIMIT_BYTES = 56 * 1024 * 1024

COND_ROWS = 16
CONV_PAD = 16
ROW_TILE = 512
FFN_ROW_TILE = 1024
FFN_SPLIT = 4
CONV_ROWS = 256
CF_FRONT_TILES = 2
GDN_TILE = 256
GDN_BATCH_GROUP = 4
ATT_Q_TILE = 1024
ATT_K_TILE = 256
ATT_Q_TILE_ROWMAX = 256
ATT_K_TILE_ROWMAX = 512
LOG2_E = 1.4426950408889634
ATT_SAFE_SCORE_BOUND = 40.0


def _sigmoid(x):
    return 1.0 / (1.0 + jnp.exp(-x))


def _silu(x):
    return x * _sigmoid(x)


def _softplus(x):
    return jnp.maximum(x, 0.0) + jnp.log1p(jnp.exp(-jnp.abs(x)))


def _layer_norm(x, g, b):
    mu = jnp.mean(x, axis=-1, keepdims=True)
    xc = x - mu
    var = jnp.mean(xc * xc, axis=-1, keepdims=True)
    return xc * lax.rsqrt(var + LN_EPS) * g + b


def _rms_norm(x, g):
    return x * lax.rsqrt(jnp.mean(x * x, axis=-1, keepdims=True) + RMS_EPS) * g


def _bdot(a, b):
    return jnp.dot(a.astype(BF16), b.astype(BF16), preferred_element_type=F32)


def _bdot_nt(a, b):
    return lax.dot_general(a.astype(BF16), b.astype(BF16), (((1,), (1,)), ((), ())),
                           preferred_element_type=F32)


def _params(*semantics):
    return pltpu.CompilerParams(dimension_semantics=semantics,
                                vmem_limit_bytes=V7X_VMEM_LIMIT_BYTES)


def _resident(shape):
    nd = len(shape)
    return pl.BlockSpec(shape, lambda *_: (0,) * nd, pipeline_mode=pl.Buffered(1))


def _row_tile(m, seq, tile=ROW_TILE):
    t = min(tile, seq)
    assert m % t == 0 and seq % t == 0
    return t


def _ada_kernel(c_ref, w_ref, b_ref, o_ref):
    c = c_ref[...]
    o_ref[0] = _bdot(_silu(c), w_ref[0]) + b_ref[0]


def _ada(cond, ada_w, ada_b):
    depth, d, nd = ada_w.shape
    return pl.pallas_call(
        _ada_kernel,
        grid=(depth, nd // d),
        in_specs=[pl.BlockSpec((COND_ROWS, d), lambda l, j: (0, 0)),
                  pl.BlockSpec((1, d, d), lambda l, j: (l, 0, j)),
                  pl.BlockSpec((1, 1, d), lambda l, j: (l, 0, j))],
        out_specs=pl.BlockSpec((1, COND_ROWS, d), lambda l, j: (l, 0, j)),
        out_shape=jax.ShapeDtypeStruct((depth, COND_ROWS, nd), F32),
        compiler_params=_params("parallel", "parallel"),
        name="ada_modulation",
    )(cond, ada_w, ada_b.reshape(depth, 1, nd))


def _ffn_kernel(x_ref, mods_ref, *refs, mod_base, d_ff, alpha, mixer_proj):
    if mixer_proj:
        a_ref, wo_ref, lng1_ref, lnb1_ref = refs[:4]
        refs = refs[4:]
    wgu_ref, wd_ref, lng_ref, lnb_ref, o_ref = refs
    shift = mods_ref[0, mod_base:mod_base + 1, :]
    scale = mods_ref[0, mod_base + 1:mod_base + 2, :]
    gate = mods_ref[0, mod_base + 2:mod_base + 3, :]
    tm = x_ref.shape[0]
    halves = [slice(r, r + tm // FFN_SPLIT) for r in range(0, tm, tm // FFN_SPLIT)]
    xs = [x_ref[rows, :] for rows in halves]
    if mixer_proj:
        mixes = [jnp.dot(a_ref[rows, :], wo_ref[...], preferred_element_type=F32) for rows in halves]
        xs = [_layer_norm(alpha * x + mods_ref[0, 5:6, :] * mix, lng1_ref[...], lnb1_ref[...])
              for x, mix in zip(xs, mixes)]
    hs = [(x * (1.0 + scale) + shift).astype(BF16) for x in xs]
    gs = [jnp.dot(h, wgu_ref[:, :d_ff], preferred_element_type=F32) for h in hs]
    us = [jnp.dot(h, wgu_ref[:, d_ff:], preferred_element_type=F32) for h in hs]
    acts = [(_silu(g) * u).astype(BF16) for g, u in zip(gs, us)]
    ys = [jnp.dot(a, wd_ref[...], preferred_element_type=F32) for a in acts]
    for rows, x, y in zip(halves, xs, ys):
        o_ref[rows, :] = _layer_norm(alpha * x + (MACARON_WEIGHT * gate) * y, lng_ref[...], lnb_ref[...])


def _ffn(x2, mods, seq, w_gu, w_down, which, ln_g, ln_b, mod_base, alpha, mixer_proj=None):
    m, d = x2.shape
    d_ff = w_down.shape[2]
    tm = _row_tile(m, seq if mods.shape[0] > 1 else m, FFN_ROW_TILE)
    rpm = seq // tm if mods.shape[0] > 1 else m // tm
    pick = lambda shape: pl.BlockSpec((None, None) + shape, lambda i: which + (0, 0),
                                      pipeline_mode=pl.Buffered(1))
    row = lambda w: pl.BlockSpec((tm, w), lambda i: (i, 0))
    pre_specs, pre_args = [], []
    if mixer_proj is not None:
        a2, w_out, ln_g1, ln_b1 = mixer_proj
        pre_specs = [row(a2.shape[1]), _resident(w_out.shape), _resident((1, d)), _resident((1, d))]
        pre_args = [a2, w_out, ln_g1.reshape(1, d), ln_b1.reshape(1, d)]
    return pl.pallas_call(
        functools.partial(_ffn_kernel, mod_base=mod_base, d_ff=d_ff, alpha=alpha,
                          mixer_proj=mixer_proj is not None),
        grid=(m // tm,),
        in_specs=[row(d), pl.BlockSpec((1, N_MOD, d), lambda i: (i // rpm, 0, 0))] + pre_specs
                 + [pick((d, 2 * d_ff)), pick((d_ff, d)), _resident((1, d)), _resident((1, d))],
        out_specs=row(d),
        out_shape=jax.ShapeDtypeStruct((m, d), F32),
        compiler_params=_params("parallel"),
        name="ffn_sublayer" if mixer_proj is None else "attn_out_proj_ffn",
    )(x2, mods, *pre_args, w_gu, w_down, ln_g.reshape(1, d), ln_b.reshape(1, d))


def _dw_conv_rows(pad_ref, w_ref, j, row0, rows, w_tile=None):
    taps = w_ref.shape[0]
    w_tile = j if w_tile is None else w_tile
    lanes = slice(w_tile * V7X_LANES, (w_tile + 1) * V7X_LANES)
    base = CONV_PAD + row0 - taps // 2
    acc = w_ref[0:1, lanes] * pad_ref[j, base:base + rows, :]
    for k in range(1, taps):
        acc = acc + w_ref[k:k + 1, lanes] * pad_ref[j, base + k:base + k + rows, :]
    return acc


def _fill_conv_pad(pad_ref, j, val, first, last):
    tm = val.shape[0] - 2 * CONV_PAD
    pad_ref[j, 0:CONV_PAD, :] = jnp.where(first, 0.0, val[0:CONV_PAD])
    pad_ref[j, CONV_PAD:CONV_PAD + tm, :] = val[CONV_PAD:CONV_PAD + tm]
    pad_ref[j, CONV_PAD + tm:, :] = jnp.where(last, 0.0, val[CONV_PAD + tm:])


def _even_front_kernel(xp_ref, x_ref, xn_ref, mods_ref, w_ref, alog_ref, dtb_ref, wq_ref, wc_ref, bc_ref,
                       qkv_ref, z_ref, gb_ref, yf_ref, u_ref, qpad_ref, upad_ref,
                       *, n_qkv, n_z, n_cf, tiles_per_seq, conv_rows):
    tm = x_ref.shape[0]
    i = pl.program_id(0)
    first = i % tiles_per_seq == 0
    last = i % tiles_per_seq == tiles_per_seq - 1
    shift = mods_ref[0, 3:4, :]
    scale = mods_ref[0, 4:5, :]
    x = jnp.concatenate([xp_ref[...], x_ref[...], xn_ref[...]], axis=0)
    h = (x * (1.0 + scale) + shift).astype(BF16)
    body = slice(CONV_PAD, CONV_PAD + tm)
    proj = lambda c0, n: jnp.dot(h, w_ref[:, c0:c0 + n], preferred_element_type=F32)

    o_cf = n_qkv + n_z + V7X_LANES
    n_front = CF_FRONT_TILES * V7X_LANES
    uf = proj(o_cf, n_front) * _sigmoid(proj(o_cf + n_cf, n_front))
    for j in range(CF_FRONT_TILES):
        lanes = slice(j * V7X_LANES, (j + 1) * V7X_LANES)
        _fill_conv_pad(upad_ref, j, uf[:, lanes], first, last)
        for r0 in range(0, tm, conv_rows):
            yf_ref[r0:r0 + conv_rows, lanes] = (_dw_conv_rows(upad_ref, wc_ref, j, r0, conv_rows)
                                                + bc_ref[:, lanes])

    group = 2 * V7X_LANES
    for c0 in range(0, n_qkv, group):
        yq = proj(c0, group)
        for jj in range(group // V7X_LANES):
            j = c0 // V7X_LANES + jj
            lanes = slice(j * V7X_LANES, (j + 1) * V7X_LANES)
            _fill_conv_pad(qpad_ref, j, yq[:, jj * V7X_LANES:(jj + 1) * V7X_LANES], first, last)
            for r0 in range(0, tm, conv_rows):
                t = _silu(_dw_conv_rows(qpad_ref, wq_ref, j, r0, conv_rows))
                if j < 2 * GDN_HEADS:
                    t = t * lax.rsqrt(jnp.sum(t * t, axis=-1, keepdims=True) + RMS_EPS)
                    if j < GDN_HEADS:
                        t = t * (GDN_HEAD_DIM ** -0.5)
                qkv_ref[r0:r0 + conv_rows, lanes] = t

    hb = h[body]
    za = jnp.dot(hb, w_ref[:, n_qkv:o_cf], preferred_element_type=F32)
    z_ref[...] = za[:, :n_z]
    ab = za[:, n_z:]
    u_ref[...] = (jnp.dot(hb, w_ref[:, o_cf + n_front:o_cf + n_cf], preferred_element_type=F32)
                  * _sigmoid(jnp.dot(hb, w_ref[:, o_cf + n_cf + n_front:], preferred_element_type=F32)))
    lane = lax.broadcasted_iota(jnp.int32, ab.shape, 1)
    g = -jnp.exp(alog_ref[...]) * _softplus(ab + dtb_ref[...])
    gb_ref[...] = jnp.where(lane < 2 * GDN_HEADS, g, _sigmoid(ab))


def _even_front(x2, mods, seq, w_pad, alog_row, dtb_row, w_qkv_conv, w_cf_conv, b_cf_conv, n_qkv, n_z, n_cf):
    m, d = x2.shape
    tm = _row_tile(m, seq)
    rpm = seq // tm if mods.shape[0] > 1 else m // tm
    hpt = tm // CONV_PAD
    n_halo = m // CONV_PAD
    assert max(w_qkv_conv.shape[0], w_cf_conv.shape[0]) // 2 <= CONV_PAD and tm % CONV_PAD == 0
    n_front = CF_FRONT_TILES * V7X_LANES
    row = lambda w: pl.BlockSpec((tm, w), lambda i: (i, 0))
    return pl.pallas_call(
        functools.partial(_even_front_kernel, n_qkv=n_qkv, n_z=n_z, n_cf=n_cf, tiles_per_seq=seq // tm,
                          conv_rows=min(CONV_ROWS, tm)),
        grid=(m // tm,),
        in_specs=[pl.BlockSpec((CONV_PAD, d), lambda i: (jnp.maximum(i * hpt - 1, 0), 0)),
                  row(d),
                  pl.BlockSpec((CONV_PAD, d), lambda i: (jnp.minimum((i + 1) * hpt, n_halo - 1), 0)),
                  pl.BlockSpec((1, N_MOD, d), lambda i: (i // rpm, 0, 0)),
                  _resident(w_pad.shape), _resident((1, V7X_LANES)), _resident((1, V7X_LANES)),
                  _resident(w_qkv_conv.shape), _resident(w_cf_conv.shape), _resident((1, n_cf))],
        out_specs=[row(n_qkv), row(n_z), row(V7X_LANES), row(n_front), row(n_cf - n_front)],
        out_shape=[jax.ShapeDtypeStruct((m, n_qkv), F32), jax.ShapeDtypeStruct((m, n_z), F32),
                   jax.ShapeDtypeStruct((m, V7X_LANES), F32), jax.ShapeDtypeStruct((m, n_front), F32),
                   jax.ShapeDtypeStruct((m, n_cf - n_front), F32)],
        scratch_shapes=[pltpu.VMEM((n_qkv // V7X_LANES, tm + 2 * CONV_PAD, V7X_LANES), F32),
                        pltpu.VMEM((CF_FRONT_TILES, tm + 2 * CONV_PAD, V7X_LANES), F32)],
        compiler_params=_params("parallel"),
        name="even_front",
    )(x2, x2, x2, mods, w_pad, alog_row, dtb_row, w_qkv_conv, w_cf_conv, b_cf_conv.reshape(1, n_cf))


def _gdn_kernel(qkvf_ref, qkvb_ref, gbf_ref, gbb_ref, s0_ref, of_ref, ob_ref, sout_ref, s_ref,
                *, n_chunks, n_batch):
    nh, dh, ck = GDN_HEADS, GDN_HEAD_DIM, GDN_CHUNK
    width = nh * dh
    t = pl.program_id(1)

    @pl.when(t == 0)
    def _():
        s_ref[...] = s0_ref[...]

    row = lax.broadcasted_iota(jnp.int32, (ck, ck), 0)
    col = lax.broadcasted_iota(jnp.int32, (ck, ck), 1)
    eye = (row == col).astype(F32)
    masks = ((row >= col, row > col), (row <= col, row < col))
    tri = [jnp.where(m[0], 1.0, 0.0).astype(BF16) for m in masks]
    n_double = ck.bit_length() - 2

    dirs = ((qkvf_ref, gbf_ref, of_ref), (qkvb_ref, gbb_ref, ob_ref))
    chains = [(bb, d, h) for bb in range(n_batch) for d in range(2) for h in range(nh)]

    def chunk_step(ci, carry):
        r0s = [pl.multiple_of((ci if d == 0 else n_chunks - 1 - ci) * ck, ck) for d in range(2)]
        gbcs, gcs, gcts = {}, {}, {}
        for bb in range(n_batch):
            for d, (_, gb_ref, _) in enumerate(dirs):
                gbc = gb_ref[bb, pl.ds(r0s[d], ck), :]
                hi = gbc.astype(BF16)
                r1 = gbc - hi.astype(F32)
                mid = r1.astype(BF16)
                lo = (r1 - mid.astype(F32)).astype(BF16)
                parts = jnp.dot(tri[d], jnp.concatenate([hi, mid, lo], axis=1),
                                preferred_element_type=F32)
                gc_all = (parts[:, :V7X_LANES] + parts[:, V7X_LANES:2 * V7X_LANES]
                          + parts[:, 2 * V7X_LANES:])
                gbcs[bb, d], gcs[bb, d], gcts[bb, d] = gbc, gc_all, gc_all.T

        st = []
        for bb, d, h in chains:
            lane = d * nh + h
            qkv_ref = dirs[d][0]
            mask = masks[d][0]
            rows = pl.ds(r0s[d], ck)
            gcol = gcs[bb, d][:, lane:lane + 1]
            grow = gcts[bb, d][lane:lane + 1, :]
            beta = gbcs[bb, d][:, 2 * nh + lane:2 * nh + lane + 1]
            q = qkv_ref[bb, rows, h * dh:(h + 1) * dh]
            k = qkv_ref[bb, rows, width + h * dh:width + (h + 1) * dh]
            v = qkv_ref[bb, rows, 2 * width + h * dh:2 * width + (h + 1) * dh]
            decay = jnp.where(mask, jnp.exp(jnp.where(mask, gcol - grow, 0.0)), 0.0)
            kb = k * beta
            eg = jnp.exp(gcol)
            last = ck - 1 if d == 0 else 0
            glast = gcol[last:last + 1, :]
            st.append(dict(bb=bb, d=d, h=h, lane=lane, rows=rows, q=q, k=k, decay=decay, kb=kb, eg=eg,
                           glast=glast, gcol=gcol,
                           rhs=jnp.concatenate([v * beta, kb * eg], axis=1)))
        for c in st:
            c["kkqk"] = _bdot_nt(jnp.concatenate([c["kb"], c["q"]], axis=0), c["k"])
        for c in st:
            c["mp"] = jnp.where(masks[c["d"]][1], c["kkqk"][:ck] * c["decay"], 0.0)
            c["attn"] = c["kkqk"][ck:] * c["decay"]
            c["inv"] = eye - c["mp"]
        for c in st:
            c["mp"] = _bdot(c["mp"], c["mp"])
        for _ in range(n_double - 1):
            for c in st:
                both = _bdot(jnp.concatenate([c["inv"], c["mp"]], axis=0), c["mp"])
                c["inv"] = c["inv"] + both[:ck]
                c["mp"] = both[ck:]
        for c in st:
            c["inv"] = c["inv"] + _bdot(c["inv"], c["mp"])
        for c in st:
            c["uw"] = _bdot(c["inv"], c["rhs"])
        for c in st:
            c["s"] = s_ref[c["bb"], c["lane"]]
            c["wq_s"] = _bdot(jnp.concatenate([c["uw"][:, dh:], c["q"] * c["eg"]], axis=0), c["s"])
        for c in st:
            c["v_new"] = c["uw"][:, :dh] - c["wq_s"][:ck]
            kt = c["k"] * jnp.exp(c["glast"] - c["gcol"])
            c["av"] = _bdot(jnp.concatenate([c["attn"], kt.T], axis=0), c["v_new"])
        for c in st:
            s_ref[c["bb"], c["lane"]] = c["s"] * jnp.exp(c["glast"]) + c["av"][ck:]
            dirs[c["d"]][2][c["bb"], c["rows"], c["h"] * dh:(c["h"] + 1) * dh] = (c["wq_s"][ck:]
                                                                                + c["av"][:ck])
        return carry

    lax.fori_loop(0, n_chunks, chunk_step, 0)

    @pl.when(t == pl.num_programs(1) - 1)
    def _():
        sout_ref[...] = s_ref[...]


def _gdn_scan(qkv3, gb3, s0):
    b, seq, w3 = qkv3.shape
    nh, dh = GDN_HEADS, GDN_HEAD_DIM
    lt = min(GDN_TILE, seq)
    nb = GDN_BATCH_GROUP if b % GDN_BATCH_GROUP == 0 else 1
    assert seq % lt == 0 and lt % GDN_CHUNK == 0
    nt = seq // lt
    fwd = lambda i, t: (i, t, 0)
    bwd = lambda i, t: (i, nt - 1 - t, 0)
    st = pl.BlockSpec((nb, 2 * nh, dh, dh), lambda i, t: (i, 0, 0, 0))
    return pl.pallas_call(
        functools.partial(_gdn_kernel, n_chunks=lt // GDN_CHUNK, n_batch=nb),
        grid=(b // nb, nt),
        in_specs=[pl.BlockSpec((nb, lt, w3), fwd), pl.BlockSpec((nb, lt, w3), bwd),
                  pl.BlockSpec((nb, lt, V7X_LANES), fwd), pl.BlockSpec((nb, lt, V7X_LANES), bwd), st],
        out_specs=[pl.BlockSpec((nb, lt, nh * dh), fwd), pl.BlockSpec((nb, lt, nh * dh), bwd), st],
        out_shape=[jax.ShapeDtypeStruct((b, seq, nh * dh), F32),
                   jax.ShapeDtypeStruct((b, seq, nh * dh), F32),
                   jax.ShapeDtypeStruct((b, 2 * nh, dh, dh), F32)],
        scratch_shapes=[pltpu.VMEM((nb, 2 * nh, dh, dh), F32)],
        compiler_params=_params("parallel", "arbitrary"),
        name="gdn_scan",
    )(qkv3, qkv3, gb3, gb3, s0)


def _even_merge_kernel(of_ref, ob_ref, z_ref, yf_ref, up_ref, u_ref, un_ref, x_ref, mods_ref, onorm_ref, wc_ref,
                       bc_ref, clng_ref, clnb_ref, w1_ref, w2_ref, lng_ref, lnb_ref, o_ref, upad_ref, y_ref,
                       *, alpha, tiles_per_seq, conv_rows):
    dh = GDN_HEAD_DIM
    tm = u_ref.shape[0]
    i = pl.program_id(0)
    first = i % tiles_per_seq == 0
    last = i % tiles_per_seq == tiles_per_seq - 1
    n_front = yf_ref.shape[1]
    y_ref[:, 0:n_front] = yf_ref[...]
    u = jnp.concatenate([up_ref[...], u_ref[...], un_ref[...]], axis=0)
    for j in range(upad_ref.shape[0]):
        _fill_conv_pad(upad_ref, j, u[:, j * V7X_LANES:(j + 1) * V7X_LANES], first, last)
        jw = CF_FRONT_TILES + j
        lanes = slice(jw * V7X_LANES, (jw + 1) * V7X_LANES)
        for r0 in range(0, tm, conv_rows):
            y_ref[r0:r0 + conv_rows, lanes] = (_dw_conv_rows(upad_ref, wc_ref, j, r0, conv_rows, jw)
                                               + bc_ref[:, lanes])
    o = of_ref[...] + ob_ref[...]
    on = jnp.concatenate([_rms_norm(o[:, h * dh:(h + 1) * dh], onorm_ref[...])
                          for h in range(GDN_HEADS)], axis=1)
    a = on * _silu(z_ref[...])
    cf = _silu(_layer_norm(y_ref[...], clng_ref[...], clnb_ref[...]))
    mix = _bdot(a, w1_ref[...]) + _bdot(cf, w2_ref[...])
    gate = mods_ref[0, 5:6, :]
    o_ref[...] = _layer_norm(alpha * x_ref[...] + gate * mix, lng_ref[...], lnb_ref[...])


def _even_merge(of2, ob2, z2, yf2, u2, x2, mods, seq, onorm, w_cf_conv, b_cf_conv, clng, clnb, w1, w2, ln_g,
                ln_b, alpha):
    m, d = x2.shape
    n = of2.shape[1]
    nf, nr = yf2.shape[1], u2.shape[1]
    tm = _row_tile(m, seq)
    rpm = seq // tm if mods.shape[0] > 1 else m // tm
    hpt = tm // CONV_PAD
    n_halo = m // CONV_PAD
    assert w_cf_conv.shape[0] // 2 <= CONV_PAD and tm % CONV_PAD == 0
    row = lambda w: pl.BlockSpec((tm, w), lambda i: (i, 0))
    return pl.pallas_call(
        functools.partial(_even_merge_kernel, alpha=alpha, tiles_per_seq=seq // tm,
                          conv_rows=min(CONV_ROWS, tm)),
        grid=(m // tm,),
        in_specs=[row(n), row(n), row(n), row(nf),
                  pl.BlockSpec((CONV_PAD, nr), lambda i: (jnp.maximum(i * hpt - 1, 0), 0)),
                  row(nr),
                  pl.BlockSpec((CONV_PAD, nr), lambda i: (jnp.minimum((i + 1) * hpt, n_halo - 1), 0)),
                  row(d),
                  pl.BlockSpec((1, N_MOD, d), lambda i: (i // rpm, 0, 0)),
                  _resident((1, GDN_HEAD_DIM)), _resident(w_cf_conv.shape), _resident((1, n)),
                  _resident((1, n)), _resident((1, n)),
                  _resident(w1.shape), _resident(w2.shape), _resident((1, d)), _resident((1, d))],
        out_specs=row(d),
        out_shape=jax.ShapeDtypeStruct((m, d), F32),
        scratch_shapes=[pltpu.VMEM((nr // V7X_LANES, tm + 2 * CONV_PAD, V7X_LANES), F32),
                        pltpu.VMEM((tm, n), F32)],
        compiler_params=_params("parallel"),
        name="even_merge_out_proj",
    )(of2, ob2, z2, yf2, u2, u2, u2, x2, mods, onorm.reshape(1, -1), w_cf_conv, b_cf_conv.reshape(1, n),
      clng.reshape(1, n), clnb.reshape(1, n), w1, w2, ln_g.reshape(1, d), ln_b.reshape(1, d))


def _attn_proj_kernel(x_ref, mods_ref, w_ref, qn_ref, kn_ref, cos_ref, sin_ref, q_ref, k_ref, v_ref,
                      y_ref, *, rope):
    hd = ATT_HEAD_DIM
    x = x_ref[...]
    shift = mods_ref[0, 3:4, :]
    scale = mods_ref[0, 4:5, :]
    h = (x * (1.0 + scale) + shift).astype(BF16)

    ones = jnp.ones((2 * hd, hd), BF16)

    def head(t, gain, post):
        sq = t * t
        hi = sq.astype(BF16)
        lo = (sq - hi.astype(F32)).astype(BF16)
        ss = jnp.dot(jnp.concatenate([hi, lo], axis=1), ones, preferred_element_type=F32)
        t = t * lax.rsqrt(ss * (1.0 / hd) + RMS_EPS) * gain
        if rope:
            t = t * cos_ref[...] + pltpu.roll(t, hd // 2, axis=1) * sin_ref[...]
        return (t * post).astype(BF16)

    pair = 2 * hd
    n_q, n_kv = ATT_HEADS * hd, ATT_KV_HEADS * hd
    for c0 in range(0, n_q + n_kv, pair):
        y_ref[:, c0:c0 + pair] = jnp.dot(h, w_ref[:, c0:c0 + pair], preferred_element_type=F32)
    v_ref[...] = jnp.dot(h, w_ref[:, n_q + n_kv:], preferred_element_type=F32).astype(BF16)
    for c in range(0, n_q + n_kv, hd):
        t = y_ref[:, c:c + hd]
        if c < n_q:
            q_ref[:, c:c + hd] = head(t, qn_ref[...], hd ** -0.5 * LOG2_E)
        else:
            k_ref[:, c - n_q:c - n_q + hd] = head(t, kn_ref[...], 1.0)


def _attn_proj(x2, mods, seq, w, qn, kn, cos, sin, rope):
    m, d = x2.shape
    hd = ATT_HEAD_DIM
    tm = _row_tile(m, seq)
    rpm = seq // tm if mods.shape[0] > 1 else m // tm
    ntab = cos.shape[0] // tm
    row = lambda w_: pl.BlockSpec((tm, w_), lambda i: (i, 0))
    tab = pl.BlockSpec((tm, hd), lambda i: (i % ntab, 0))
    return pl.pallas_call(
        functools.partial(_attn_proj_kernel, rope=rope),
        grid=(m // tm,),
        in_specs=[row(d), pl.BlockSpec((1, N_MOD, d), lambda i: (i // rpm, 0, 0)),
                  _resident(w.shape), _resident((1, hd)), _resident((1, hd)), tab, tab],
        out_specs=[row(ATT_HEADS * hd), row(ATT_KV_HEADS * hd), row(ATT_KV_HEADS * hd)],
        out_shape=[jax.ShapeDtypeStruct((m, ATT_HEADS * hd), BF16),
                   jax.ShapeDtypeStruct((m, ATT_KV_HEADS * hd), BF16),
                   jax.ShapeDtypeStruct((m, ATT_KV_HEADS * hd), BF16)],
        scratch_shapes=[pltpu.VMEM((tm, (ATT_HEADS + ATT_KV_HEADS) * hd), F32)],
        compiler_params=_params("parallel"),
        name="attn_in_proj",
    )(x2, mods, w, qn.reshape(1, hd), kn.reshape(1, hd), cos, sin)


def _attn_kernel(q_ref, kl_ref, vl_ref, kc_ref, vc_ref, o_ref, s_ref, vx_ref, *, tk, subtract_max):
    hd = ATT_HEAD_DIM
    seq, ct = kl_ref.shape[1], kc_ref.shape[1]
    tq = q_ref.shape[1]

    @pl.when(pl.program_id(2) == 0)
    def _():
        vx_ref[0:seq, 0:hd] = vl_ref[0]
        vx_ref[seq:seq + ct, 0:hd] = vc_ref[0]
        vx_ref[:, hd:2 * hd] = jnp.ones((seq + ct, hd), BF16)

    chunks = [(kl_ref, j * tk, min(tk, seq - j * tk), j * tk) for j in range(pl.cdiv(seq, tk))]
    chunks += [(kc_ref, j * tk, min(tk, ct - j * tk), seq + j * tk) for j in range(pl.cdiv(ct, tk))]

    if not subtract_max:
        q4 = jnp.concatenate([q_ref[0, :, g * hd:(g + 1) * hd] for g in range(ATT_GROUP)], axis=0)
        acc = jnp.zeros((ATT_GROUP * tq, 2 * hd), F32)
        for k_ref, start, size, off in chunks:
            p = jnp.exp2(_bdot_nt(q4, k_ref[0, start:start + size, :])).astype(BF16)
            acc = acc + jnp.dot(p, vx_ref[off:off + size, :], preferred_element_type=F32)
        out = (acc[:, :hd] / acc[:, hd:hd + 1]).astype(BF16)
        for g in range(ATT_GROUP):
            o_ref[0, :, g * hd:(g + 1) * hd] = out[g * tq:(g + 1) * tq]
        return

    for g in range(ATT_GROUP):
        q = q_ref[0, :, g * hd:(g + 1) * hd]
        buf = g % 2
        mrun = jnp.full((tq, V7X_LANES), -jnp.inf, F32)
        for k_ref, start, size, off in chunks:
            s = _bdot_nt(q, k_ref[0, start:start + size, :])
            s_ref[buf, :, off:off + size] = s
            for c in range(size // V7X_LANES):
                mrun = jnp.maximum(mrun, s[:, c * V7X_LANES:(c + 1) * V7X_LANES])
        mx = jnp.max(mrun, axis=-1, keepdims=True)
        acc = jnp.zeros((tq, 2 * hd), F32)
        for _, _, size, off in chunks:
            p = jnp.exp2(s_ref[buf, :, off:off + size] - mx).astype(BF16)
            acc = acc + jnp.dot(p, vx_ref[off:off + size, :], preferred_element_type=F32)
        o_ref[0, :, g * hd:(g + 1) * hd] = (acc[:, :hd] / acc[:, hd:hd + 1]).astype(BF16)


def _attention_call(q3, kl3, vl3, kc3, vc3, subtract_max):
    b, seq, _ = q3.shape
    ct = kc3.shape[1]
    hd = ATT_HEAD_DIM
    tq = min(ATT_Q_TILE_ROWMAX if subtract_max else ATT_Q_TILE, seq)
    tk = min(ATT_K_TILE_ROWMAX if subtract_max else ATT_K_TILE, seq)
    assert seq % V7X_LANES == 0 and ct % V7X_LANES == 0
    gw = ATT_GROUP * hd
    kv = lambda n: pl.BlockSpec((1, n, hd), lambda i, g, t: (i, 0, g))
    s_shape = (2, tq, seq + ct) if subtract_max else (2, V7X_SUBLANES, V7X_LANES)
    return pl.pallas_call(
        functools.partial(_attn_kernel, tk=tk, subtract_max=subtract_max),
        grid=(b, ATT_KV_HEADS, seq // tq),
        in_specs=[pl.BlockSpec((1, tq, gw), lambda i, g, t: (i, t, g)),
                  kv(seq), kv(seq), kv(ct), kv(ct)],
        out_specs=pl.BlockSpec((1, tq, gw), lambda i, g, t: (i, t, g)),
        out_shape=jax.ShapeDtypeStruct((b, seq, ATT_HEADS * hd), BF16),
        scratch_shapes=[pltpu.VMEM(s_shape, F32), pltpu.VMEM((seq + ct, 2 * hd), BF16)],
        compiler_params=_params("parallel", "parallel", "arbitrary"),
        name="gqa_attention" if subtract_max else "gqa_attention_bounded",
    )(q3, kl3, vl3, kc3, vc3)


def _attention(q3, kl3, vl3, kc3, vc3, score_bound):
    args = (q3, kl3, vl3, kc3, vc3)
    return lax.cond(score_bound < ATT_SAFE_SCORE_BOUND,
                    lambda *a: _attention_call(*a, subtract_max=False),
                    lambda *a: _attention_call(*a, subtract_max=True), *args)


def _rope_tables(seq):
    rows = seq // GRID_W
    row = jnp.repeat(jnp.arange(rows, dtype=F32), GRID_W)
    col = jnp.tile(jnp.arange(GRID_W, dtype=F32), rows)
    n_freq = ATT_HEAD_DIM // 4
    inv_freq = jnp.float32(ROPE_THETA) ** (-jnp.arange(n_freq, dtype=F32) / n_freq)
    ang = jnp.concatenate([row[:, None] * inv_freq, col[:, None] * inv_freq], axis=-1)
    cos, sin = jnp.cos(ang), jnp.sin(ang)
    return jnp.concatenate([cos, cos], axis=-1), jnp.concatenate([-sin, sin], axis=-1)


def _split_pairs(n_heads):
    hd = ATT_HEAD_DIM
    within = jnp.concatenate([jnp.arange(0, hd, 2), jnp.arange(1, hd, 2)])
    return (jnp.arange(n_heads)[:, None] * hd + within[None, :]).reshape(-1)


def _lane_row(v):
    flat = v.astype(F32).reshape(-1)
    return jnp.zeros((1, V7X_LANES), F32).at[0, :flat.shape[0]].set(flat)


def kernel(x, c, ctx, c_ctx, ada_w, ada_b, ln_g, ln_b, ffn_w_gu, ffn_w_down, even_w_in, even_qkv_conv,
           gdn_a_log, gdn_dt_bias, gdn_out_norm, cf_dw_conv, cf_dw_bias, cf_ln_g, cf_ln_b, even_w_out,
           attn_w_in, attn_q_norm, attn_k_norm, attn_w_out):
    b, seq, d = x.shape
    ct = ctx.shape[1]
    depth = ada_w.shape[0]
    alpha = (2.0 * depth) ** 0.25
    nh, dh = GDN_HEADS, GDN_HEAD_DIM
    gw = nh * dh
    assert b + 1 <= COND_ROWS and depth <= 2

    cond = jnp.zeros((COND_ROWS, d), F32).at[:b].set(c).at[b].set(c_ctx)
    mods = _ada(cond, ada_w, ada_b)
    xl = x.reshape(b * seq, d)
    xc = ctx.reshape(b * ct, d)
    wgu, wdn = ffn_w_gu.astype(BF16), ffn_w_down.astype(BF16)

    for layer in range(depth):
        last = layer == depth - 1
        i = layer // 2
        ml = mods[layer, :b].reshape(b, N_MOD, d)
        mc = mods[layer, b:b + 1].reshape(1, N_MOD, d)
        lg, lb = ln_g[layer], ln_b[layer]

        xl = _ffn(xl, ml, seq, wgu, wdn, (layer, 0), lg[0], lb[0], 0, alpha)
        xc = _ffn(xc, mc, ct, wgu, wdn, (layer, 0), lg[0], lb[0], 0, alpha)

        mixer_proj = None
        if layer % 2 == 0:
            w_in = even_w_in[i]
            n_ab = 4 * nh
            n_cf = (w_in.shape[1] - 4 * gw - n_ab) // 2
            w_pad = jnp.concatenate(
                [w_in[:, :4 * gw],
                 jnp.pad(w_in[:, 4 * gw:4 * gw + n_ab], ((0, 0), (0, V7X_LANES - n_ab))),
                 w_in[:, 4 * gw + n_ab:]], axis=1).astype(BF16)
            alog_row, dtb_row = _lane_row(gdn_a_log[i]), _lane_row(gdn_dt_bias[i])
            w_out = even_w_out[i].astype(BF16)

            def mixer_in(x2, m_, n_seq):
                qkv, z, gb, yf, u = _even_front(x2, m_, n_seq, w_pad, alog_row, dtb_row, even_qkv_conv[i],
                                                cf_dw_conv[i], cf_dw_bias[i], 3 * gw, gw, n_cf)
                return qkv.reshape(b, n_seq, 3 * gw), z, gb.reshape(b, n_seq, V7X_LANES), (yf, u)

            def mixer_out(of, ob, z, yu, x2, m_, n_seq):
                return _even_merge(of.reshape(-1, gw), ob.reshape(-1, gw), z, yu[0], yu[1], x2, m_, n_seq,
                                   gdn_out_norm[i], cf_dw_conv[i], cf_dw_bias[i], cf_ln_g[i], cf_ln_b[i],
                                   w_out[:gw], w_out[gw:], lg[1], lb[1], alpha)

            qkv_c, z_c, gb_c, y_c = mixer_in(xc, mc, ct)
            qkv_l, z_l, gb_l, y_l = mixer_in(xl, ml, seq)
            of_c, ob_c, s_ctx = _gdn_scan(qkv_c, gb_c, jnp.zeros((b, 2 * nh, dh, dh), F32))
            of_l, ob_l, _ = _gdn_scan(qkv_l, gb_l, s_ctx)
            xl = mixer_out(of_l, ob_l, z_l, y_l, xl, ml, seq)
            if not last:
                xc = mixer_out(of_c, ob_c, z_c, y_c, xc, mc, ct)
        else:
            assert last, "context attention output is only needed before a further layer"
            hd = ATT_HEAD_DIM
            nq, nkv = ATT_HEADS * hd, ATT_KV_HEADS * hd
            w_in = attn_w_in[i]
            w_perm = jnp.concatenate([w_in[:, :nq][:, _split_pairs(ATT_HEADS)],
                                      w_in[:, nq:nq + nkv][:, _split_pairs(ATT_KV_HEADS)],
                                      w_in[:, nq + nkv:]], axis=1).astype(BF16)
            within = _split_pairs(1)
            qn, kn = attn_q_norm[i][within], attn_k_norm[i][within]
            cos, sin = _rope_tables(seq)
            q_l, k_l, v_l = _attn_proj(xl, ml, seq, w_perm, qn, kn, cos, sin, True)
            _, k_c, v_c = _attn_proj(xc, mc, ct, w_perm, qn, kn, cos, sin, False)
            score_bound = (hd ** 0.5 * (1.0 + 2.0 ** -6) * jnp.max(jnp.abs(attn_q_norm[i]))
                           * jnp.max(jnp.abs(attn_k_norm[i])))
            o = _attention(q_l.reshape(b, seq, nq), k_l.reshape(b, seq, nkv), v_l.reshape(b, seq, nkv),
                           k_c.reshape(b, ct, nkv), v_c.reshape(b, ct, nkv), score_bound)
            mixer_proj = (o.reshape(b * seq, nq), attn_w_out[i].astype(BF16), lg[1], lb[1])

        xl = _ffn(xl, ml, seq, wgu, wdn, (layer, 1), lg[2], lb[2], 6, alpha, mixer_proj)
        if not last:
            xc = _ffn(xc, mc, ct, wgu, wdn, (layer, 1), lg[2], lb[2], 6, alpha)

    return xl.reshape(b, seq, d)
```

```python
import functools

import jax
import jax.numpy as jnp
from jax import lax
from jax.experimental import pallas as pl
from jax.experimental.pallas import tpu as pltpu

F32 = jnp.float32
BF16 = jnp.bfloat16

GRID_W = 64
LN_EPS = 1e-5
RMS_EPS = 1e-6
N_MOD = 9
MACARON_WEIGHT = 0.5
GDN_HEADS = 4
GDN_HEAD_DIM = 128
GDN_CHUNK = 64
ATT_HEADS = 8
ATT_KV_HEADS = 2
ATT_GROUP = ATT_HEADS // ATT_KV_HEADS
ATT_HEAD_DIM = 128
ROPE_THETA = 10000.0

V7X_LANES = 128
V7X_SUBLANES = 8
V7X_VMEM_LIMIT_BYTES = 56 * 1024 * 1024

COND_ROWS = 16
CONV_PAD = 16
ROW_TILE = 1024
FFN_ROW_TILE = 1024
FFN_SPLIT = 4
CONV_ROWS = 256
CF_FRONT_TILES = 2
GDN_TILE = 256
GDN_BATCH_GROUP = 4
ATT_Q_TILE = 1024
ATT_K_TILE = 256
ATT_Q_TILE_ROWMAX = 256
ATT_K_TILE_ROWMAX = 512
LOG2_E = 1.4426950408889634
ATT_SAFE_SCORE_BOUND = 40.0


def _sigmoid(x):
    return 1.0 / (1.0 + jnp.exp(-x))


def _silu(x):
    return x * _sigmoid(x)


def _softplus(x):
    return jnp.maximum(x, 0.0) + jnp.log1p(jnp.exp(-jnp.abs(x)))


def _layer_norm(x, g, b):
    mu = jnp.mean(x, axis=-1, keepdims=True)
    xc = x - mu
    var = jnp.mean(xc * xc, axis=-1, keepdims=True)
    return xc * lax.rsqrt(var + LN_EPS) * g + b


def _rms_norm(x, g):
    return x * lax.rsqrt(jnp.mean(x * x, axis=-1, keepdims=True) + RMS_EPS) * g


def _bdot(a, b):
    return jnp.dot(a.astype(BF16), b.astype(BF16), preferred_element_type=F32)


def _bdot_nt(a, b):
    return lax.dot_general(a.astype(BF16), b.astype(BF16), (((1,), (1,)), ((), ())),
                           preferred_element_type=F32)


def _params(*semantics):
    return pltpu.CompilerParams(dimension_semantics=semantics,
                                vmem_limit_bytes=V7X_VMEM_LIMIT_BYTES)


def _resident(shape):
    nd = len(shape)
    return pl.BlockSpec(shape, lambda *_: (0,) * nd, pipeline_mode=pl.Buffered(1))


def _row_tile(m, seq, tile=ROW_TILE):
    t = min(tile, seq)
    assert m % t == 0 and seq % t == 0
    return t


def _ada_kernel(c_ref, w_ref, b_ref, o_ref):
    c = c_ref[...]
    o_ref[0] = _bdot(_silu(c), w_ref[0]) + b_ref[0]


def _ada(cond, ada_w, ada_b):
    depth, d, nd = ada_w.shape
    return pl.pallas_call(
        _ada_kernel,
        grid=(depth, nd // d),
        in_specs=[pl.BlockSpec((COND_ROWS, d), lambda l, j: (0, 0)),
                  pl.BlockSpec((1, d, d), lambda l, j: (l, 0, j)),
                  pl.BlockSpec((1, 1, d), lambda l, j: (l, 0, j))],
        out_specs=pl.BlockSpec((1, COND_ROWS, d), lambda l, j: (l, 0, j)),
        out_shape=jax.ShapeDtypeStruct((depth, COND_ROWS, nd), F32),
        compiler_params=_params("parallel", "parallel"),
        name="ada_modulation",
    )(cond, ada_w, ada_b.reshape(depth, 1, nd))


def _ffn_kernel(x_ref, mods_ref, *refs, mod_base, d_ff, alpha, mixer_proj):
    if mixer_proj:
        a_ref, wo_ref, lng1_ref, lnb1_ref = refs[:4]
        refs = refs[4:]
    wgu_ref, wd_ref, lng_ref, lnb_ref, o_ref = refs
    shift = mods_ref[0, mod_base:mod_base + 1, :]
    scale = mods_ref[0, mod_base + 1:mod_base + 2, :]
    gate = mods_ref[0, mod_base + 2:mod_base + 3, :]
    tm = x_ref.shape[0]
    halves = [slice(r, r + tm // FFN_SPLIT) for r in range(0, tm, tm // FFN_SPLIT)]
    xs = [x_ref[rows, :] for rows in halves]
    if mixer_proj:
        mixes = [jnp.dot(a_ref[rows, :], wo_ref[...], preferred_element_type=F32) for rows in halves]
        xs = [_layer_norm(alpha * x + mods_ref[0, 5:6, :] * mix, lng1_ref[...], lnb1_ref[...])
              for x, mix in zip(xs, mixes)]
    hs = [(x * (1.0 + scale) + shift).astype(BF16) for x in xs]
    gs = [jnp.dot(h, wgu_ref[:, :d_ff], preferred_element_type=F32) for h in hs]
    us = [jnp.dot(h, wgu_ref[:, d_ff:], preferred_element_type=F32) for h in hs]
    acts = [(_silu(g) * u).astype(BF16) for g, u in zip(gs, us)]
    ys = [jnp.dot(a, wd_ref[...], preferred_element_type=F32) for a in acts]
    for rows, x, y in zip(halves, xs, ys):
        o_ref[rows, :] = _layer_norm(alpha * x + (MACARON_WEIGHT * gate) * y, lng_ref[...], lnb_ref[...])


def _ffn(x2, mods, seq, w_gu, w_down, which, ln_g, ln_b, mod_base, alpha, mixer_proj=None):
    m, d = x2.shape
    d_ff = w_down.shape[2]
    tm = _row_tile(m, seq if mods.shape[0] > 1 else m, FFN_ROW_TILE)
    rpm = seq // tm if mods.shape[0] > 1 else m // tm
    pick = lambda shape: pl.BlockSpec((None, None) + shape, lambda i: which + (0, 0),
                                      pipeline_mode=pl.Buffered(1))
    row = lambda w: pl.BlockSpec((tm, w), lambda i: (i, 0))
    pre_specs, pre_args = [], []
    if mixer_proj is not None:
        a2, w_out, ln_g1, ln_b1 = mixer_proj
        pre_specs = [row(a2.shape[1]), _resident(w_out.shape), _resident((1, d)), _resident((1, d))]
        pre_args = [a2, w_out, ln_g1.reshape(1, d), ln_b1.reshape(1, d)]
    return pl.pallas_call(
        functools.partial(_ffn_kernel, mod_base=mod_base, d_ff=d_ff, alpha=alpha,
                          mixer_proj=mixer_proj is not None),
        grid=(m // tm,),
        in_specs=[row(d), pl.BlockSpec((1, N_MOD, d), lambda i: (i // rpm, 0, 0))] + pre_specs
                 + [pick((d, 2 * d_ff)), pick((d_ff, d)), _resident((1, d)), _resident((1, d))],
        out_specs=row(d),
        out_shape=jax.ShapeDtypeStruct((m, d), F32),
        compiler_params=_params("parallel"),
        name="ffn_sublayer" if mixer_proj is None else "attn_out_proj_ffn",
    )(x2, mods, *pre_args, w_gu, w_down, ln_g.reshape(1, d), ln_b.reshape(1, d))


def _dw_conv_rows(pad_ref, w_ref, j, row0, rows, w_tile=None):
    taps = w_ref.shape[0]
    w_tile = j if w_tile is None else w_tile
    lanes = slice(w_tile * V7X_LANES, (w_tile + 1) * V7X_LANES)
    base = CONV_PAD + row0 - taps // 2
    acc = w_ref[0:1, lanes] * pad_ref[j, base:base + rows, :]
    for k in range(1, taps):
        acc = acc + w_ref[k:k + 1, lanes] * pad_ref[j, base + k:base + k + rows, :]
    return acc


def _fill_conv_pad(pad_ref, j, val, first, last):
    tm = val.shape[0] - 2 * CONV_PAD
    pad_ref[j, 0:CONV_PAD, :] = jnp.where(first, 0.0, val[0:CONV_PAD])
    pad_ref[j, CONV_PAD:CONV_PAD + tm, :] = val[CONV_PAD:CONV_PAD + tm]
    pad_ref[j, CONV_PAD + tm:, :] = jnp.where(last, 0.0, val[CONV_PAD + tm:])


def _even_front_kernel(xp_ref, x_ref, xn_ref, mods_ref, w_ref, alog_ref, dtb_ref, wq_ref, wc_ref, bc_ref,
                       qkv_ref, z_ref, gb_ref, yf_ref, u_ref, qpad_ref, upad_ref,
                       *, n_qkv, n_z, n_cf, tiles_per_seq, conv_rows):
    tm = x_ref.shape[0]
    i = pl.program_id(0)
    first = i % tiles_per_seq == 0
    last = i % tiles_per_seq == tiles_per_seq - 1
    shift = mods_ref[0, 3:4, :]
    scale = mods_ref[0, 4:5, :]
    x = jnp.concatenate([xp_ref[...], x_ref[...], xn_ref[...]], axis=0)
    h = (x * (1.0 + scale) + shift).astype(BF16)
    body = slice(CONV_PAD, CONV_PAD + tm)
    proj = lambda c0, n: jnp.dot(h, w_ref[:, c0:c0 + n], preferred_element_type=F32)

    o_cf = n_qkv + n_z + V7X_LANES
    n_front = CF_FRONT_TILES * V7X_LANES
    uf = proj(o_cf, n_front) * _sigmoid(proj(o_cf + n_cf, n_front))
    for j in range(CF_FRONT_TILES):
        lanes = slice(j * V7X_LANES, (j + 1) * V7X_LANES)
        _fill_conv_pad(upad_ref, j, uf[:, lanes], first, last)
        for r0 in range(0, tm, conv_rows):
            yf_ref[r0:r0 + conv_rows, lanes] = (_dw_conv_rows(upad_ref, wc_ref, j, r0, conv_rows)
                                                + bc_ref[:, lanes])

    group = 2 * V7X_LANES
    for c0 in range(0, n_qkv, group):
        yq = proj(c0, group)
        for jj in range(group // V7X_LANES):
            j = c0 // V7X_LANES + jj
            lanes = slice(j * V7X_LANES, (j + 1) * V7X_LANES)
            _fill_conv_pad(qpad_ref, j, yq[:, jj * V7X_LANES:(jj + 1) * V7X_LANES], first, last)
            for r0 in range(0, tm, conv_rows):
                t = _silu(_dw_conv_rows(qpad_ref, wq_ref, j, r0, conv_rows))
                if j < 2 * GDN_HEADS:
                    t = t * lax.rsqrt(jnp.sum(t * t, axis=-1, keepdims=True) + RMS_EPS)
                    if j < GDN_HEADS:
                        t = t * (GDN_HEAD_DIM ** -0.5)
                qkv_ref[r0:r0 + conv_rows, lanes] = t

    hb = h[body]
    za = jnp.dot(hb, w_ref[:, n_qkv:o_cf], preferred_element_type=F32)
    z_ref[...] = za[:, :n_z]
    ab = za[:, n_z:]
    u_ref[...] = (jnp.dot(hb, w_ref[:, o_cf + n_front:o_cf + n_cf], preferred_element_type=F32)
                  * _sigmoid(jnp.dot(hb, w_ref[:, o_cf + n_cf + n_front:], preferred_element_type=F32)))
    lane = lax.broadcasted_iota(jnp.int32, ab.shape, 1)
    g = -jnp.exp(alog_ref[...]) * _softplus(ab + dtb_ref[...])
    gb_ref[...] = jnp.where(lane < 2 * GDN_HEADS, g, _sigmoid(ab))


def _even_front(x2, mods, seq, w_pad, alog_row, dtb_row, w_qkv_conv, w_cf_conv, b_cf_conv, n_qkv, n_z, n_cf):
    m, d = x2.shape
    tm = _row_tile(m, seq)
    rpm = seq // tm if mods.shape[0] > 1 else m // tm
    hpt = tm // CONV_PAD
    n_halo = m // CONV_PAD
    assert max(w_qkv_conv.shape[0], w_cf_conv.shape[0]) // 2 <= CONV_PAD and tm % CONV_PAD == 0
    n_front = CF_FRONT_TILES * V7X_LANES
    row = lambda w: pl.BlockSpec((tm, w), lambda i: (i, 0))
    return pl.pallas_call(
        functools.partial(_even_front_kernel, n_qkv=n_qkv, n_z=n_z, n_cf=n_cf, tiles_per_seq=seq // tm,
                          conv_rows=min(CONV_ROWS, tm)),
        grid=(m // tm,),
        in_specs=[pl.BlockSpec((CONV_PAD, d), lambda i: (jnp.maximum(i * hpt - 1, 0), 0)),
                  row(d),
                  pl.BlockSpec((CONV_PAD, d), lambda i: (jnp.minimum((i + 1) * hpt, n_halo - 1), 0)),
                  pl.BlockSpec((1, N_MOD, d), lambda i: (i // rpm, 0, 0)),
                  _resident(w_pad.shape), _resident((1, V7X_LANES)), _resident((1, V7X_LANES)),
                  _resident(w_qkv_conv.shape), _resident(w_cf_conv.shape), _resident((1, n_cf))],
        out_specs=[row(n_qkv), row(n_z), row(V7X_LANES), row(n_front), row(n_cf - n_front)],
        out_shape=[jax.ShapeDtypeStruct((m, n_qkv), F32), jax.ShapeDtypeStruct((m, n_z), F32),
                   jax.ShapeDtypeStruct((m, V7X_LANES), F32), jax.ShapeDtypeStruct((m, n_front), F32),
                   jax.ShapeDtypeStruct((m, n_cf - n_front), F32)],
        scratch_shapes=[pltpu.VMEM((n_qkv // V7X_LANES, tm + 2 * CONV_PAD, V7X_LANES), F32),
                        pltpu.VMEM((CF_FRONT_TILES, tm + 2 * CONV_PAD, V7X_LANES), F32)],
        compiler_params=_params("parallel"),
        name="even_front",
    )(x2, x2, x2, mods, w_pad, alog_row, dtb_row, w_qkv_conv, w_cf_conv, b_cf_conv.reshape(1, n_cf))


def _gdn_kernel(qkvf_ref, qkvb_ref, gbf_ref, gbb_ref, s0_ref, of_ref, ob_ref, sout_ref, s_ref,
                *, n_chunks, n_batch):
    nh, dh, ck = GDN_HEADS, GDN_HEAD_DIM, GDN_CHUNK
    width = nh * dh
    t = pl.program_id(1)

    @pl.when(t == 0)
    def _():
        s_ref[...] = s0_ref[...]

    row = lax.broadcasted_iota(jnp.int32, (ck, ck), 0)
    col = lax.broadcasted_iota(jnp.int32, (ck, ck), 1)
    eye = (row == col).astype(F32)
    masks = ((row >= col, row > col), (row <= col, row < col))
    tri = [jnp.where(m[0], 1.0, 0.0).astype(BF16) for m in masks]
    n_double = ck.bit_length() - 2

    dirs = ((qkvf_ref, gbf_ref, of_ref), (qkvb_ref, gbb_ref, ob_ref))
    chains = [(bb, d, h) for bb in range(n_batch) for d in range(2) for h in range(nh)]

    def chunk_step(ci, carry):
        r0s = [pl.multiple_of((ci if d == 0 else n_chunks - 1 - ci) * ck, ck) for d in range(2)]
        gbcs, gcs, gcts = {}, {}, {}
        for bb in range(n_batch):
            for d, (_, gb_ref, _) in enumerate(dirs):
                gbc = gb_ref[bb, pl.ds(r0s[d], ck), :]
                hi = gbc.astype(BF16)
                r1 = gbc - hi.astype(F32)
                mid = r1.astype(BF16)
                lo = (r1 - mid.astype(F32)).astype(BF16)
                parts = jnp.dot(tri[d], jnp.concatenate([hi, mid, lo], axis=1),
                                preferred_element_type=F32)
                gc_all = (parts[:, :V7X_LANES] + parts[:, V7X_LANES:2 * V7X_LANES]
                          + parts[:, 2 * V7X_LANES:])
                gbcs[bb, d], gcs[bb, d], gcts[bb, d] = gbc, gc_all, gc_all.T

        st = []
        for bb, d, h in chains:
            lane = d * nh + h
            qkv_ref = dirs[d][0]
            mask = masks[d][0]
            rows = pl.ds(r0s[d], ck)
            gcol = gcs[bb, d][:, lane:lane + 1]
            grow = gcts[bb, d][lane:lane + 1, :]
            beta = gbcs[bb, d][:, 2 * nh + lane:2 * nh + lane + 1]
            q = qkv_ref[bb, rows, h * dh:(h + 1) * dh]
            k = qkv_ref[bb, rows, width + h * dh:width + (h + 1) * dh]
            v = qkv_ref[bb, rows, 2 * width + h * dh:2 * width + (h + 1) * dh]
            decay = jnp.where(mask, jnp.exp(jnp.where(mask, gcol - grow, 0.0)), 0.0)
            kb = k * beta
            eg = jnp.exp(gcol)
            last = ck - 1 if d == 0 else 0
            glast = gcol[last:last + 1, :]
            st.append(dict(bb=bb, d=d, h=h, lane=lane, rows=rows, q=q, k=k, decay=decay, kb=kb, eg=eg,
                           glast=glast, gcol=gcol,
                           rhs=jnp.concatenate([v * beta, kb * eg], axis=1)))
        for c in st:
            c["kkqk"] = _bdot_nt(jnp.concatenate([c["kb"], c["q"]], axis=0), c["k"])
        for c in st:
            c["mp"] = jnp.where(masks[c["d"]][1], c["kkqk"][:ck] * c["decay"], 0.0)
            c["attn"] = c["kkqk"][ck:] * c["decay"]
            c["inv"] = eye - c["mp"]
        for c in st:
            c["mp"] = _bdot(c["mp"], c["mp"])
        for _ in range(n_double - 1):
            for c in st:
                both = _bdot(jnp.concatenate([c["inv"], c["mp"]], axis=0), c["mp"])
                c["inv"] = c["inv"] + both[:ck]
                c["mp"] = both[ck:]
        for c in st:
            c["inv"] = c["inv"] + _bdot(c["inv"], c["mp"])
        for c in st:
            c["uw"] = _bdot(c["inv"], c["rhs"])
        for c in st:
            c["s"] = s_ref[c["bb"], c["lane"]]
            c["wq_s"] = _bdot(jnp.concatenate([c["uw"][:, dh:], c["q"] * c["eg"]], axis=0), c["s"])
        for c in st:
            c["v_new"] = c["uw"][:, :dh] - c["wq_s"][:ck]
            kt = c["k"] * jnp.exp(c["glast"] - c["gcol"])
            c["av"] = _bdot(jnp.concatenate([c["attn"], kt.T], axis=0), c["v_new"])
        for c in st:
            s_ref[c["bb"], c["lane"]] = c["s"] * jnp.exp(c["glast"]) + c["av"][ck:]
            dirs[c["d"]][2][c["bb"], c["rows"], c["h"] * dh:(c["h"] + 1) * dh] = (c["wq_s"][ck:]
                                                                                + c["av"][:ck])
        return carry

    lax.fori_loop(0, n_chunks, chunk_step, 0)

    @pl.when(t == pl.num_programs(1) - 1)
    def _():
        sout_ref[...] = s_ref[...]


def _gdn_scan(qkv3, gb3, s0):
    b, seq, w3 = qkv3.shape
    nh, dh = GDN_HEADS, GDN_HEAD_DIM
    lt = min(GDN_TILE, seq)
    nb = GDN_BATCH_GROUP if b % GDN_BATCH_GROUP == 0 else 1
    assert seq % lt == 0 and lt % GDN_CHUNK == 0
    nt = seq // lt
    fwd = lambda i, t: (i, t, 0)
    bwd = lambda i, t: (i, nt - 1 - t, 0)
    st = pl.BlockSpec((nb, 2 * nh, dh, dh), lambda i, t: (i, 0, 0, 0))
    return pl.pallas_call(
        functools.partial(_gdn_kernel, n_chunks=lt // GDN_CHUNK, n_batch=nb),
        grid=(b // nb, nt),
        in_specs=[pl.BlockSpec((nb, lt, w3), fwd), pl.BlockSpec((nb, lt, w3), bwd),
                  pl.BlockSpec((nb, lt, V7X_LANES), fwd), pl.BlockSpec((nb, lt, V7X_LANES), bwd), st],
        out_specs=[pl.BlockSpec((nb, lt, nh * dh), fwd), pl.BlockSpec((nb, lt, nh * dh), bwd), st],
        out_shape=[jax.ShapeDtypeStruct((b, seq, nh * dh), F32),
                   jax.ShapeDtypeStruct((b, seq, nh * dh), F32),
                   jax.ShapeDtypeStruct((b, 2 * nh, dh, dh), F32)],
        scratch_shapes=[pltpu.VMEM((nb, 2 * nh, dh, dh), F32)],
        compiler_params=_params("parallel", "arbitrary"),
        name="gdn_scan",
    )(qkv3, qkv3, gb3, gb3, s0)


def _even_merge_kernel(of_ref, ob_ref, z_ref, yf_ref, up_ref, u_ref, un_ref, x_ref, mods_ref, onorm_ref, wc_ref,
                       bc_ref, clng_ref, clnb_ref, w1_ref, w2_ref, lng_ref, lnb_ref, o_ref, upad_ref, y_ref,
                       *, alpha, tiles_per_seq, conv_rows):
    dh = GDN_HEAD_DIM
    tm = u_ref.shape[0]
    i = pl.program_id(0)
    first = i % tiles_per_seq == 0
    last = i % tiles_per_seq == tiles_per_seq - 1
    n_front = yf_ref.shape[1]
    y_ref[:, 0:n_front] = yf_ref[...]
    u = jnp.concatenate([up_ref[...], u_ref[...], un_ref[...]], axis=0)
    for j in range(upad_ref.shape[0]):
        _fill_conv_pad(upad_ref, j, u[:, j * V7X_LANES:(j + 1) * V7X_LANES], first, last)
        jw = CF_FRONT_TILES + j
        lanes = slice(jw * V7X_LANES, (jw + 1) * V7X_LANES)
        for r0 in range(0, tm, conv_rows):
            y_ref[r0:r0 + conv_rows, lanes] = (_dw_conv_rows(upad_ref, wc_ref, j, r0, conv_rows, jw)
                                               + bc_ref[:, lanes])
    o = of_ref[...] + ob_ref[...]
    on = jnp.concatenate([_rms_norm(o[:, h * dh:(h + 1) * dh], onorm_ref[...])
                          for h in range(GDN_HEADS)], axis=1)
    a = on * _silu(z_ref[...])
    cf = _silu(_layer_norm(y_ref[...], clng_ref[...], clnb_ref[...]))
    mix = _bdot(a, w1_ref[...]) + _bdot(cf, w2_ref[...])
    gate = mods_ref[0, 5:6, :]
    o_ref[...] = _layer_norm(alpha * x_ref[...] + gate * mix, lng_ref[...], lnb_ref[...])


def _even_merge(of2, ob2, z2, yf2, u2, x2, mods, seq, onorm, w_cf_conv, b_cf_conv, clng, clnb, w1, w2, ln_g,
                ln_b, alpha):
    m, d = x2.shape
    n = of2.shape[1]
    nf, nr = yf2.shape[1], u2.shape[1]
    tm = _row_tile(m, seq)
    rpm = seq // tm if mods.shape[0] > 1 else m // tm
    hpt = tm // CONV_PAD
    n_halo = m // CONV_PAD
    assert w_cf_conv.shape[0] // 2 <= CONV_PAD and tm % CONV_PAD == 0
    row = lambda w: pl.BlockSpec((tm, w), lambda i: (i, 0))
    return pl.pallas_call(
        functools.partial(_even_merge_kernel, alpha=alpha, tiles_per_seq=seq // tm,
                          conv_rows=min(CONV_ROWS, tm)),
        grid=(m // tm,),
        in_specs=[row(n), row(n), row(n), row(nf),
                  pl.BlockSpec((CONV_PAD, nr), lambda i: (jnp.maximum(i * hpt - 1, 0), 0)),
                  row(nr),
                  pl.BlockSpec((CONV_PAD, nr), lambda i: (jnp.minimum((i + 1) * hpt, n_halo - 1), 0)),
                  row(d),
                  pl.BlockSpec((1, N_MOD, d), lambda i: (i // rpm, 0, 0)),
                  _resident((1, GDN_HEAD_DIM)), _resident(w_cf_conv.shape), _resident((1, n)),
                  _resident((1, n)), _resident((1, n)),
                  _resident(w1.shape), _resident(w2.shape), _resident((1, d)), _resident((1, d))],
        out_specs=row(d),
        out_shape=jax.ShapeDtypeStruct((m, d), F32),
        scratch_shapes=[pltpu.VMEM((nr // V7X_LANES, tm + 2 * CONV_PAD, V7X_LANES), F32),
                        pltpu.VMEM((tm, n), F32)],
        compiler_params=_params("parallel"),
        name="even_merge_out_proj",
    )(of2, ob2, z2, yf2, u2, u2, u2, x2, mods, onorm.reshape(1, -1), w_cf_conv, b_cf_conv.reshape(1, n),
      clng.reshape(1, n), clnb.reshape(1, n), w1, w2, ln_g.reshape(1, d), ln_b.reshape(1, d))


def _attn_proj_kernel(x_ref, mods_ref, w_ref, qn_ref, kn_ref, cos_ref, sin_ref, q_ref, k_ref, v_ref,
                      y_ref, *, rope):
    hd = ATT_HEAD_DIM
    x = x_ref[...]
    shift = mods_ref[0, 3:4, :]
    scale = mods_ref[0, 4:5, :]
    h = (x * (1.0 + scale) + shift).astype(BF16)

    ones = jnp.ones((2 * hd, hd), BF16)

    def head(t, gain, post):
        sq = t * t
        hi = sq.astype(BF16)
        lo = (sq - hi.astype(F32)).astype(BF16)
        ss = jnp.dot(jnp.concatenate([hi, lo], axis=1), ones, preferred_element_type=F32)
        t = t * lax.rsqrt(ss * (1.0 / hd) + RMS_EPS) * gain
        if rope:
            t = t * cos_ref[...] + pltpu.roll(t, hd // 2, axis=1) * sin_ref[...]
        return (t * post).astype(BF16)

    pair = 2 * hd
    n_q, n_kv = ATT_HEADS * hd, ATT_KV_HEADS * hd
    for c0 in range(0, n_q + n_kv, pair):
        y_ref[:, c0:c0 + pair] = jnp.dot(h, w_ref[:, c0:c0 + pair], preferred_element_type=F32)
    v_ref[...] = jnp.dot(h, w_ref[:, n_q + n_kv:], preferred_element_type=F32).astype(BF16)
    for c in range(0, n_q + n_kv, hd):
        t = y_ref[:, c:c + hd]
        if c < n_q:
            q_ref[:, c:c + hd] = head(t, qn_ref[...], hd ** -0.5 * LOG2_E)
        else:
            k_ref[:, c - n_q:c - n_q + hd] = head(t, kn_ref[...], 1.0)


def _attn_proj(x2, mods, seq, w, qn, kn, cos, sin, rope):
    m, d = x2.shape
    hd = ATT_HEAD_DIM
    tm = _row_tile(m, seq)
    rpm = seq // tm if mods.shape[0] > 1 else m // tm
    ntab = cos.shape[0] // tm
    row = lambda w_: pl.BlockSpec((tm, w_), lambda i: (i, 0))
    tab = pl.BlockSpec((tm, hd), lambda i: (i % ntab, 0))
    return pl.pallas_call(
        functools.partial(_attn_proj_kernel, rope=rope),
        grid=(m // tm,),
        in_specs=[row(d), pl.BlockSpec((1, N_MOD, d), lambda i: (i // rpm, 0, 0)),
                  _resident(w.shape), _resident((1, hd)), _resident((1, hd)), tab, tab],
        out_specs=[row(ATT_HEADS * hd), row(ATT_KV_HEADS * hd), row(ATT_KV_HEADS * hd)],
        out_shape=[jax.ShapeDtypeStruct((m, ATT_HEADS * hd), BF16),
                   jax.ShapeDtypeStruct((m, ATT_KV_HEADS * hd), BF16),
                   jax.ShapeDtypeStruct((m, ATT_KV_HEADS * hd), BF16)],
        scratch_shapes=[pltpu.VMEM((tm, (ATT_HEADS + ATT_KV_HEADS) * hd), F32)],
        compiler_params=_params("parallel"),
        name="attn_in_proj",
    )(x2, mods, w, qn.reshape(1, hd), kn.reshape(1, hd), cos, sin)


def _attn_kernel(q_ref, kl_ref, vl_ref, kc_ref, vc_ref, o_ref, s_ref, vx_ref, *, tk, subtract_max):
    hd = ATT_HEAD_DIM
    seq, ct = kl_ref.shape[1], kc_ref.shape[1]
    tq = q_ref.shape[1]

    @pl.when(pl.program_id(2) == 0)
    def _():
        vx_ref[0:seq, 0:hd] = vl_ref[0]
        vx_ref[seq:seq + ct, 0:hd] = vc_ref[0]
        vx_ref[:, hd:2 * hd] = jnp.ones((seq + ct, hd), BF16)

    chunks = [(kl_ref, j * tk, min(tk, seq - j * tk), j * tk) for j in range(pl.cdiv(seq, tk))]
    chunks += [(kc_ref, j * tk, min(tk, ct - j * tk), seq + j * tk) for j in range(pl.cdiv(ct, tk))]

    if not subtract_max:
        q4 = jnp.concatenate([q_ref[0, :, g * hd:(g + 1) * hd] for g in range(ATT_GROUP)], axis=0)
        acc = jnp.zeros((ATT_GROUP * tq, 2 * hd), F32)
        for k_ref, start, size, off in chunks:
            p = jnp.exp2(_bdot_nt(q4, k_ref[0, start:start + size, :])).astype(BF16)
            acc = acc + jnp.dot(p, vx_ref[off:off + size, :], preferred_element_type=F32)
        out = (acc[:, :hd] / acc[:, hd:hd + 1]).astype(BF16)
        for g in range(ATT_GROUP):
            o_ref[0, :, g * hd:(g + 1) * hd] = out[g * tq:(g + 1) * tq]
        return

    for g in range(ATT_GROUP):
        q = q_ref[0, :, g * hd:(g + 1) * hd]
        buf = g % 2
        mrun = jnp.full((tq, V7X_LANES), -jnp.inf, F32)
        for k_ref, start, size, off in chunks:
            s = _bdot_nt(q, k_ref[0, start:start + size, :])
            s_ref[buf, :, off:off + size] = s
            for c in range(size // V7X_LANES):
                mrun = jnp.maximum(mrun, s[:, c * V7X_LANES:(c + 1) * V7X_LANES])
        mx = jnp.max(mrun, axis=-1, keepdims=True)
        acc = jnp.zeros((tq, 2 * hd), F32)
        for _, _, size, off in chunks:
            p = jnp.exp2(s_ref[buf, :, off:off + size] - mx).astype(BF16)
            acc = acc + jnp.dot(p, vx_ref[off:off + size, :], preferred_element_type=F32)
        o_ref[0, :, g * hd:(g + 1) * hd] = (acc[:, :hd] / acc[:, hd:hd + 1]).astype(BF16)


def _attention_call(q3, kl3, vl3, kc3, vc3, subtract_max):
    b, seq, _ = q3.shape
    ct = kc3.shape[1]
    hd = ATT_HEAD_DIM
    tq = min(ATT_Q_TILE_ROWMAX if subtract_max else ATT_Q_TILE, seq)
    tk = min(ATT_K_TILE_ROWMAX if subtract_max else ATT_K_TILE, seq)
    assert seq % V7X_LANES == 0 and ct % V7X_LANES == 0
    gw = ATT_GROUP * hd
    kv = lambda n: pl.BlockSpec((1, n, hd), lambda i, g, t: (i, 0, g))
    s_shape = (2, tq, seq + ct) if subtract_max else (2, V7X_SUBLANES, V7X_LANES)
    return pl.pallas_call(
        functools.partial(_attn_kernel, tk=tk, subtract_max=subtract_max),
        grid=(b, ATT_KV_HEADS, seq // tq),
        in_specs=[pl.BlockSpec((1, tq, gw), lambda i, g, t: (i, t, g)),
                  kv(seq), kv(seq), kv(ct), kv(ct)],
        out_specs=pl.BlockSpec((1, tq, gw), lambda i, g, t: (i, t, g)),
        out_shape=jax.ShapeDtypeStruct((b, seq, ATT_HEADS * hd), BF16),
        scratch_shapes=[pltpu.VMEM(s_shape, F32), pltpu.VMEM((seq + ct, 2 * hd), BF16)],
        compiler_params=_params("parallel", "parallel", "arbitrary"),
        name="gqa_attention" if subtract_max else "gqa_attention_bounded",
    )(q3, kl3, vl3, kc3, vc3)


def _attention(q3, kl3, vl3, kc3, vc3, score_bound):
    args = (q3, kl3, vl3, kc3, vc3)
    return lax.cond(score_bound < ATT_SAFE_SCORE_BOUND,
                    lambda *a: _attention_call(*a, subtract_max=False),
                    lambda *a: _attention_call(*a, subtract_max=True), *args)


def _rope_tables(seq):
    rows = seq // GRID_W
    row = jnp.repeat(jnp.arange(rows, dtype=F32), GRID_W)
    col = jnp.tile(jnp.arange(GRID_W, dtype=F32), rows)
    n_freq = ATT_HEAD_DIM // 4
    inv_freq = jnp.float32(ROPE_THETA) ** (-jnp.arange(n_freq, dtype=F32) / n_freq)
    ang = jnp.concatenate([row[:, None] * inv_freq, col[:, None] * inv_freq], axis=-1)
    cos, sin = jnp.cos(ang), jnp.sin(ang)
    return jnp.concatenate([cos, cos], axis=-1), jnp.concatenate([-sin, sin], axis=-1)


def _split_pairs(n_heads):
    hd = ATT_HEAD_DIM
    within = jnp.concatenate([jnp.arange(0, hd, 2), jnp.arange(1, hd, 2)])
    return (jnp.arange(n_heads)[:, None] * hd + within[None, :]).reshape(-1)


def _lane_row(v):
    flat = v.astype(F32).reshape(-1)
    return jnp.zeros((1, V7X_LANES), F32).at[0, :flat.shape[0]].set(flat)


def kernel(x, c, ctx, c_ctx, ada_w, ada_b, ln_g, ln_b, ffn_w_gu, ffn_w_down, even_w_in, even_qkv_conv,
           gdn_a_log, gdn_dt_bias, gdn_out_norm, cf_dw_conv, cf_dw_bias, cf_ln_g, cf_ln_b, even_w_out,
           attn_w_in, attn_q_norm, attn_k_norm, attn_w_out):
    b, seq, d = x.shape
    ct = ctx.shape[1]
    depth = ada_w.shape[0]
    alpha = (2.0 * depth) ** 0.25
    nh, dh = GDN_HEADS, GDN_HEAD_DIM
    gw = nh * dh
    assert b + 1 <= COND_ROWS and depth <= 2

    cond = jnp.zeros((COND_ROWS, d), F32).at[:b].set(c).at[b].set(c_ctx)
    mods = _ada(cond, ada_w, ada_b)
    xl = x.reshape(b * seq, d)
    xc = ctx.reshape(b * ct, d)
    wgu, wdn = ffn_w_gu.astype(BF16), ffn_w_down.astype(BF16)

    for layer in range(depth):
        last = layer == depth - 1
        i = layer // 2
        ml = mods[layer, :b].reshape(b, N_MOD, d)
        mc = mods[layer, b:b + 1].reshape(1, N_MOD, d)
        lg, lb = ln_g[layer], ln_b[layer]

        xl = _ffn(xl, ml, seq, wgu, wdn, (layer, 0), lg[0], lb[0], 0, alpha)
        xc = _ffn(xc, mc, ct, wgu, wdn, (layer, 0), lg[0], lb[0], 0, alpha)

        mixer_proj = None
        if layer % 2 == 0:
            w_in = even_w_in[i]
            n_ab = 4 * nh
            n_cf = (w_in.shape[1] - 4 * gw - n_ab) // 2
            w_pad = jnp.concatenate(
                [w_in[:, :4 * gw],
                 jnp.pad(w_in[:, 4 * gw:4 * gw + n_ab], ((0, 0), (0, V7X_LANES - n_ab))),
                 w_in[:, 4 * gw + n_ab:]], axis=1).astype(BF16)
            alog_row, dtb_row = _lane_row(gdn_a_log[i]), _lane_row(gdn_dt_bias[i])
            w_out = even_w_out[i].astype(BF16)

            def mixer_in(x2, m_, n_seq):
                qkv, z, gb, yf, u = _even_front(x2, m_, n_seq, w_pad, alog_row, dtb_row, even_qkv_conv[i],
                                                cf_dw_conv[i], cf_dw_bias[i], 3 * gw, gw, n_cf)
                return qkv.reshape(b, n_seq, 3 * gw), z, gb.reshape(b, n_seq, V7X_LANES), (yf, u)

            def mixer_out(of, ob, z, yu, x2, m_, n_seq):
                return _even_merge(of.reshape(-1, gw), ob.reshape(-1, gw), z, yu[0], yu[1], x2, m_, n_seq,
                                   gdn_out_norm[i], cf_dw_conv[i], cf_dw_bias[i], cf_ln_g[i], cf_ln_b[i],
                                   w_out[:gw], w_out[gw:], lg[1], lb[1], alpha)

            qkv_c, z_c, gb_c, y_c = mixer_in(xc, mc, ct)
            qkv_l, z_l, gb_l, y_l = mixer_in(xl, ml, seq)
            of_c, ob_c, s_ctx = _gdn_scan(qkv_c, gb_c, jnp.zeros((b, 2 * nh, dh, dh), F32))
            of_l, ob_l, _ = _gdn_scan(qkv_l, gb_l, s_ctx)
            xl = mixer_out(of_l, ob_l, z_l, y_l, xl, ml, seq)
            if not last:
                xc = mixer_out(of_c, ob_c, z_c, y_c, xc, mc, ct)
        else:
            assert last, "context attention output is only needed before a further layer"
            hd = ATT_HEAD_DIM
            nq, nkv = ATT_HEADS * hd, ATT_KV_HEADS * hd
            w_in = attn_w_in[i]
            w_perm = jnp.concatenate([w_in[:, :nq][:, _split_pairs(ATT_HEADS)],
                                      w_in[:, nq:nq + nkv][:, _split_pairs(ATT_KV_HEADS)],
                                      w_in[:, nq + nkv:]], axis=1).astype(BF16)
            within = _split_pairs(1)
            qn, kn = attn_q_norm[i][within], attn_k_norm[i][within]
            cos, sin = _rope_tables(seq)
            q_l, k_l, v_l = _attn_proj(xl, ml, seq, w_perm, qn, kn, cos, sin, True)
            _, k_c, v_c = _attn_proj(xc, mc, ct, w_perm, qn, kn, cos, sin, False)
            score_bound = (hd ** 0.5 * (1.0 + 2.0 ** -6) * jnp.max(jnp.abs(attn_q_norm[i]))
                           * jnp.max(jnp.abs(attn_k_norm[i])))
            o = _attention(q_l.reshape(b, seq, nq), k_l.reshape(b, seq, nkv), v_l.reshape(b, seq, nkv),
                           k_c.reshape(b, ct, nkv), v_c.reshape(b, ct, nkv), score_bound)
            mixer_proj = (o.reshape(b * seq, nq), attn_w_out[i].astype(BF16), lg[1], lb[1])

        xl = _ffn(xl, ml, seq, wgu, wdn, (layer, 1), lg[2], lb[2], 6, alpha, mixer_proj)
        if not last:
            xc = _ffn(xc, mc, ct, wgu, wdn, (layer, 1), lg[2], lb[2], 6, alpha)

    return xl.reshape(b, seq, d)
```
